```python
import math
import jax, jax.numpy as jnp
from jax import lax
import numpy as np

D_MODEL = 1024
BATCH = 8
SEQ = 4096
DEPTH = 2

N_A_LAYERS = DEPTH // 2
N_B_LAYERS = DEPTH - N_A_LAYERS

A_HEADS = 16
A_HEAD_DIM = D_MODEL // A_HEADS
A_QKV_WIDTH = 3 * A_HEADS * A_HEAD_DIM
Q_BLOCK = 128

B_GROUPS = ((128, 1), (512, 4), (2048, 16))
B_N_GROUPS = len(B_GROUPS)
B_HEADS_PER_GROUP = 8
B_HEAD_DIM = 64
B_WINDOW_STEPS = 128
B_Q_WIDTH = B_N_GROUPS * B_HEADS_PER_GROUP * B_HEAD_DIM
B_OUT_WIDTH = B_HEADS_PER_GROUP * B_HEAD_DIM
B_KV_WIDTH = 2 * B_Q_WIDTH
ALIBI_TOTAL_HEADS = B_N_GROUPS * B_HEADS_PER_GROUP

D_FF = 2816
CONV_WIDTH = 3

RMS_EPS = 1e-6

kernel_name = "yoco_fox_dilated_convffn_hybrid"


def rms_norm(x, g):
    xf = x.astype(jnp.float32)
    y = xf * lax.rsqrt(jnp.mean(xf * xf, axis=-1, keepdims=True) + RMS_EPS)
    return (y * g.astype(jnp.float32)).astype(x.dtype)


def conv_ffn(h, w_up, conv_w, conv_b, w_down):
    u = h @ w_up
    c = u.shape[-1]
    u = lax.conv_general_dilated(
        u, conv_w[:, None, :].astype(u.dtype), window_strides=(1,),
        padding=[(CONV_WIDTH - 1, 0)],
        dimension_numbers=("NWC", "WIO", "NWC"),
        feature_group_count=c) + conv_b
    a, gate = jnp.split(u, 2, axis=-1)
    return (jax.nn.silu(gate) * a) @ w_down


def forgetting_attention(q, k, v, log_f):
    S = q.shape[1]
    dh = q.shape[-1]
    scale = dh ** -0.5
    c = jnp.cumsum(log_f.astype(jnp.float32), axis=1).transpose(0, 2, 1)
    outs = []
    for i in range(S // Q_BLOCK):
        q0 = i * Q_BLOCK
        q1 = q0 + Q_BLOCK
        s = jnp.einsum("bqhd,bkhd->bhqk", q[:, q0:q1], k[:, :q1]).astype(jnp.float32) * scale
        s = s + c[:, :, q0:q1, None] - c[:, :, None, :q1]
        causal = jnp.arange(q0, q1)[:, None] >= jnp.arange(q1)[None, :]
        p = jax.nn.softmax(jnp.where(causal, s, -jnp.inf), axis=-1)
        outs.append(jnp.einsum("bhqk,bkhd->bqhd", p.astype(v.dtype), v[:, :q1]))
    return jnp.concatenate(outs, axis=1)


def dilated_branch(q, k, v, dil, slopes):
    B, S, H, Dh = q.shape
    W = B_WINDOW_STEPS
    period = dil * W
    Sp = -(-S // period) * period
    nb = Sp // period
    pad = ((0, 0), (0, Sp - S), (0, 0), (0, 0))

    def blocks(t):
        return jnp.pad(t, pad).reshape(B, nb, W, dil, H, Dh)

    def with_prev(t):
        prev = jnp.pad(t, ((0, 0), (1, 0), (0, 0), (0, 0), (0, 0), (0, 0)))[:, :nb]
        return jnp.concatenate([prev, t], axis=2)

    qb = blocks(q)
    kk = with_prev(blocks(k))
    vv = with_prev(blocks(v))
    s = jnp.einsum("bnirhd,bnjrhd->bnrhij", qb, kk).astype(jnp.float32) * (Dh ** -0.5)
    qi = jnp.arange(W)[:, None]
    kj = jnp.arange(2 * W)[None, :]
    dist = qi + W - kj
    band = (dist >= 0) & (dist <= W)
    first = (jnp.arange(nb)[:, None, None] == 0) & (kj < W)[None]
    valid = band[None] & ~first
    bias = -slopes[:, None, None] * (dil * dist).astype(jnp.float32)
    s = jnp.where(valid[None, :, None, None], s + bias, -jnp.inf)
    m = jnp.max(s, axis=-1, keepdims=True)
    p = jnp.exp(s - m)
    l = jnp.sum(p, axis=-1, keepdims=True)
    o = jnp.einsum("bnrhij,bnjrhd->bnirhd", (p / l).astype(v.dtype), vv)
    lse = (m + jnp.log(l))[..., 0]
    o = o.reshape(B, Sp, H, Dh)[:, :S]
    lse = lse.transpose(0, 1, 4, 2, 3).reshape(B, Sp, H)[:, :S]
    return o, lse


def setup_inputs(seed: int = 0) -> dict:
    key = jax.random.key(seed)
    ks = jax.random.split(key, 16)
    f32 = jnp.float32

    def nrm(k, shape, fan_in):
        return jax.random.normal(k, shape, f32) * (fan_in ** -0.5)

    def gain(k, shape):
        return 1.0 + 0.05 * jax.random.normal(k, shape, f32)

    return {
        "x": jax.random.normal(ks[0], (BATCH, SEQ, D_MODEL), f32),
        "a_w_in": nrm(ks[1], (N_A_LAYERS, D_MODEL, A_QKV_WIDTH + A_HEADS), D_MODEL),
        "a_b_f": 2.0 + 0.1 * jax.random.normal(ks[2], (N_A_LAYERS, A_HEADS), f32),
        "a_w_out": nrm(ks[3], (N_A_LAYERS, A_HEADS * A_HEAD_DIM, D_MODEL), A_HEADS * A_HEAD_DIM),
        "b_w_q": nrm(ks[4], (N_B_LAYERS, D_MODEL, B_Q_WIDTH), D_MODEL),
        "b_w_out": nrm(ks[5], (N_B_LAYERS, B_OUT_WIDTH, D_MODEL), B_OUT_WIDTH),
        "kv_norm_g": gain(ks[6], (D_MODEL,)),
        "w_kv": nrm(ks[7], (D_MODEL, B_KV_WIDTH), D_MODEL),
        "mix_norm_g": gain(ks[8], (DEPTH, D_MODEL)),
        "ffn_norm_g": gain(ks[9], (DEPTH, D_MODEL)),
        "ffn_w_up": nrm(ks[10], (DEPTH, D_MODEL, 2 * D_FF), D_MODEL),
        "ffn_conv_w": nrm(ks[11], (DEPTH, CONV_WIDTH, 2 * D_FF), CONV_WIDTH),
        "ffn_conv_b": 0.02 * jax.random.normal(ks[12], (DEPTH, 2 * D_FF), f32),
        "ffn_w_down": nrm(ks[13], (DEPTH, D_FF, D_MODEL), D_FF),
        "final_norm_g": gain(ks[14], (D_MODEL,)),
    }


def reference(x, a_w_in, a_b_f, a_w_out, b_w_q, b_w_out, kv_norm_g, w_kv,
              mix_norm_g, ffn_norm_g, ffn_w_up, ffn_conv_w, ffn_conv_b, ffn_w_down,
              final_norm_g):
    B, S, D = x.shape
    slopes = jnp.exp2(-8.0 * jnp.arange(1, ALIBI_TOTAL_HEADS + 1, dtype=jnp.float32)
                      / ALIBI_TOTAL_HEADS).reshape(B_N_GROUPS, B_HEADS_PER_GROUP)
    kv = None
    for layer in range(DEPTH):
        if layer < N_A_LAYERS:
            h = rms_norm(x, mix_norm_g[layer])
            proj = h @ a_w_in[layer]
            qkv = proj[..., :A_QKV_WIDTH].reshape(B, S, 3, A_HEADS, A_HEAD_DIM)
            log_f = jax.nn.log_sigmoid(
                proj[..., A_QKV_WIDTH:].astype(jnp.float32) + a_b_f[layer].astype(jnp.float32))
            o = forgetting_attention(qkv[:, :, 0], qkv[:, :, 1], qkv[:, :, 2], log_f)
            x = x + o.reshape(B, S, A_HEADS * A_HEAD_DIM) @ a_w_out[layer]
        else:
            if kv is None:
                kv = (rms_norm(x, kv_norm_g) @ w_kv).reshape(
                    B, S, 2, B_N_GROUPS, B_HEADS_PER_GROUP, B_HEAD_DIM)
            bl = layer - N_A_LAYERS
            h = rms_norm(x, mix_norm_g[layer])
            q = (h @ b_w_q[bl]).reshape(B, S, B_N_GROUPS, B_HEADS_PER_GROUP, B_HEAD_DIM)
            outs, lses = [], []
            for g, (window, dil) in enumerate(B_GROUPS):
                o_g, lse_g = dilated_branch(q[:, :, g], kv[:, :, 0, g], kv[:, :, 1, g],
                                            dil, slopes[g])
                outs.append(o_g)
                lses.append(lse_g)
            alpha = jax.nn.softmax(jnp.stack(lses, axis=0), axis=0)
            o = jnp.sum(alpha[..., None].astype(outs[0].dtype) * jnp.stack(outs, axis=0), axis=0)
            x = x + o.reshape(B, S, B_OUT_WIDTH) @ b_w_out[bl]
        h = rms_norm(x, ffn_norm_g[layer])
        x = x + conv_ffn(h, ffn_w_up[layer], ffn_conv_w[layer], ffn_conv_b[layer],
                         ffn_w_down[layer])
    return rms_norm(x, final_norm_g)
```

```python
import functools

import jax
import jax.numpy as jnp
from jax import lax
from jax.experimental import pallas as pl
from jax.experimental.pallas import tpu as pltpu

F32 = jnp.float32
BF16 = jnp.bfloat16

RMS_EPS = 1e-6
HEAD_DIM = 64
LANES = 128
A_HEADS = 16
B_GROUP_DILATIONS = (1, 4, 16)
B_HEADS_PER_GROUP = 8
B_WINDOW_STEPS = 128
CONV_WIDTH = 3
NEG_BIG = -1e30
VMEM_LIMIT_BYTES = 56 * 1024 * 1024


def _params(*semantics):
    return pltpu.CompilerParams(dimension_semantics=semantics,
                                vmem_limit_bytes=VMEM_LIMIT_BYTES)


def _rms_norm(x, g):
    y = x * lax.rsqrt(jnp.mean(x * x, axis=-1, keepdims=True) + RMS_EPS)
    return y * g


def _dot(a, b):
    return jnp.dot(a, b, preferred_element_type=F32)


def _dot_nt(a, b):
    return lax.dot_general(a, b, (((1,), (1,)), ((), ())), preferred_element_type=F32)


def _norm_proj_kernel(x_ref, g_ref, w_ref, *rest, scaled_tiles, scale, with_gate):
    if with_gate:
        wf_ref, bf_ref, o_ref, logf_ref, h_ref = rest
    else:
        o_ref, h_ref = rest
    j = pl.program_id(1)

    @pl.when(j == 0)
    def _():
        h = _rms_norm(x_ref[...], g_ref[...])
        h_ref[...] = h.astype(BF16)
        if with_gate:
            z = _dot(h_ref[...], wf_ref[...]) + bf_ref[...]
            logf_ref[...] = jnp.minimum(z, 0.0) - jnp.log(1.0 + jnp.exp(-jnp.abs(z)))

    acc = _dot(h_ref[...], w_ref[...])
    if scaled_tiles:
        acc = acc * jnp.where(j < scaled_tiles, scale, 1.0).astype(F32)
    o_ref[...] = acc.astype(o_ref.dtype)


def norm_proj(x, g, w, *, tm, tn, scaled_tiles=0, scale=1.0, gate=None):
    T, D = x.shape
    N = w.shape[1]
    in_specs = [
        pl.BlockSpec((tm, D), lambda i, j: (i, 0)),
        pl.BlockSpec((1, D), lambda i, j: (0, 0)),
        pl.BlockSpec((D, tn), lambda i, j: (0, j)),
    ]
    out_shape = [jax.ShapeDtypeStruct((T, N), BF16)]
    out_specs = [pl.BlockSpec((tm, tn), lambda i, j: (i, j))]
    args = [x, g, w]
    if gate is not None:
        wf, bf = gate
        in_specs += [pl.BlockSpec((D, LANES), lambda i, j: (0, 0)),
                     pl.BlockSpec((1, LANES), lambda i, j: (0, 0))]
        out_shape.append(jax.ShapeDtypeStruct((T, LANES), F32))
        out_specs.append(pl.BlockSpec((tm, LANES), lambda i, j: (i, 0)))
        args += [wf, bf]
    kern = functools.partial(_norm_proj_kernel, scaled_tiles=scaled_tiles, scale=scale,
                             with_gate=gate is not None)
    res = pl.pallas_call(
        kern,
        grid=(T // tm, N // tn),
        in_specs=in_specs,
        out_specs=out_specs,
        out_shape=out_shape,
        scratch_shapes=[pltpu.VMEM((tm, D), BF16)],
        compiler_params=_params("parallel", "arbitrary"),
        name="norm_proj_gate" if gate is not None else "norm_proj",
    )(*args)
    return res if gate is not None else res[0]


def _cumsum_kernel(x_ref, o_ref):
    rows, n = x_ref.shape
    r = lax.broadcasted_iota(jnp.int32, (LANES, LANES), 0)
    c = lax.broadcasted_iota(jnp.int32, (LANES, LANES), 1)
    upper = jnp.where(r <= c, 1.0, 0.0).astype(F32)
    carry = jnp.zeros((rows, 1), F32)
    for k in range(n // LANES):
        blk = x_ref[:, k * LANES:(k + 1) * LANES]
        cs = jnp.dot(blk, upper, preferred_element_type=F32,
                     precision=lax.Precision.HIGHEST)
        o_ref[:, k * LANES:(k + 1) * LANES] = cs + carry
        carry = carry + cs[:, LANES - 1:LANES]


def cumsum_lanes(x):
    return pl.pallas_call(
        _cumsum_kernel,
        out_shape=jax.ShapeDtypeStruct(x.shape, F32),
        compiler_params=pltpu.CompilerParams(vmem_limit_bytes=VMEM_LIMIT_BYTES),
        name="cumsum",
    )(x)


def _fox_kernel(q_ref, k_ref, v_ref, cq_ref, ck_ref, o_ref, *, tq):
    i = pl.program_id(2)
    q = q_ref[0]
    lane = lax.broadcasted_iota(jnp.int32, (tq, LANES), 1)
    row = lax.broadcasted_iota(jnp.int32, (tq, tq), 0)
    col = lax.broadcasted_iota(jnp.int32, (tq, tq), 1)
    zero = jnp.zeros_like(q)
    outs = []
    for hh in range(2):
        in_head = (lane < HEAD_DIM) if hh == 0 else (lane >= HEAD_DIM)
        qm = jnp.where(in_head, q, zero)
        cq = cq_ref[0, 0][:, hh:hh + 1]

        def scores(j):
            start = pl.multiple_of(j * tq, tq)
            ks = k_ref[0, pl.ds(start, tq), :]
            vs = v_ref[0, pl.ds(start, tq), :]
            ck = ck_ref[0, 0, j][hh:hh + 1, :]
            return _dot_nt(qm, ks) + cq - ck, vs

        def update(carry, s, vs):
            m, l, acc = carry
            m_new = jnp.maximum(m, jnp.max(s, axis=-1, keepdims=True))
            alpha = jnp.exp(m - m_new)
            p = jnp.exp(s - m_new)
            l = alpha * l + jnp.sum(p, axis=-1, keepdims=True)
            acc = alpha * acc + _dot(p.astype(BF16), vs)
            return m_new, l, acc

        def body(j, carry):
            s, vs = scores(j)
            return update(carry, s, vs)

        init = (jnp.full((tq, 1), NEG_BIG, F32), jnp.zeros((tq, 1), F32),
                jnp.zeros((tq, LANES), F32))
        carry = lax.fori_loop(0, i, body, init)
        s, vs = scores(i)
        s = jnp.where(row >= col, s, NEG_BIG)
        m, l, acc = update(carry, s, vs)
        outs.append(acc / l)
    o_ref[0] = jnp.where(lane < HEAD_DIM, outs[0], outs[1]).astype(o_ref.dtype)


def fox_attention(qkv, c_col, c_row, *, tq):
    B, S, W3 = qkv.shape
    npair = W3 // 3 // LANES
    nq = S // tq
    return pl.pallas_call(
        functools.partial(_fox_kernel, tq=tq),
        grid=(B, npair, nq),
        in_specs=[
            pl.BlockSpec((1, tq, LANES), lambda b, h, i: (b, i, h)),
            pl.BlockSpec((1, S, LANES), lambda b, h, i: (b, 0, npair + h)),
            pl.BlockSpec((1, S, LANES), lambda b, h, i: (b, 0, 2 * npair + h)),
            pl.BlockSpec((1, 1, tq, 2), lambda b, h, i: (b, h, i, 0)),
            pl.BlockSpec((1, 1, nq, 2, tq), lambda b, h, i: (b, h, 0, 0, 0)),
        ],
        out_specs=pl.BlockSpec((1, tq, LANES), lambda b, h, i: (b, i, h)),
        out_shape=jax.ShapeDtypeStruct((B, S, npair * LANES), BF16),
        compiler_params=_params("parallel", "parallel", "arbitrary"),
        name="fox_attention",
    )(qkv, qkv, qkv, c_col, c_row)


def _proj_residual_kernel(a_ref, w_ref, x_ref, o_ref):
    o_ref[...] = x_ref[...] + _dot(a_ref[...], w_ref[...])


def proj_residual(a, w, x, *, tm):
    T, K = a.shape
    D = w.shape[1]
    return pl.pallas_call(
        _proj_residual_kernel,
        grid=(T // tm,),
        in_specs=[pl.BlockSpec((tm, K), lambda i: (i, 0)),
                  pl.BlockSpec((K, D), lambda i: (0, 0)),
                  pl.BlockSpec((tm, D), lambda i: (i, 0))],
        out_specs=pl.BlockSpec((tm, D), lambda i: (i, 0)),
        out_shape=jax.ShapeDtypeStruct((T, D), F32),
        compiler_params=_params("parallel"),
        name="proj_residual",
    )(a, w, x)


FFN_CHUNK = 256


def _shift_rows(u, prev, shift):
    rolled = pltpu.roll(u, shift, axis=0)
    head = jnp.concatenate([prev, u[:8]], axis=0)
    top = pltpu.roll(head, shift, axis=0)[8:]
    return jnp.concatenate([top, rolled[8:]], axis=0)


def _ffn_kernel(x_ref, g_ref, wup_ref, cw_ref, cb_ref, wd_ref, *rest,
                d_ff, tiles_per_seq, final_norm):
    if final_norm:
        gf_ref, o_ref, h_ref, act_ref, carry_ref = rest
    else:
        o_ref, h_ref, act_ref, carry_ref = rest
    tm = x_ref.shape[0]
    first = (pl.program_id(0) % tiles_per_seq) == 0

    @pl.when(pl.program_id(0) == 0)
    def _():
        carry_ref[...] = jnp.zeros_like(carry_ref)

    x = x_ref[...]
    h_ref[...] = _rms_norm(x, g_ref[...]).astype(BF16)

    def conv_branch(col):
        sl = slice(col, col + FFN_CHUNK)
        u = _dot(h_ref[...], wup_ref[:, sl])
        prev = jnp.where(first, 0.0, carry_ref[:, sl])
        carry_ref[:, sl] = u[tm - 8:]
        u1 = _shift_rows(u, prev, 1)
        u2 = _shift_rows(u, prev, 2)
        return (cw_ref[0:1, sl] * u2 + cw_ref[1:2, sl] * u1 + cw_ref[2:3, sl] * u
                + cb_ref[:, sl])

    for c in range(d_ff // FFN_CHUNK):
        a = conv_branch(c * FFN_CHUNK)
        gate = conv_branch(d_ff + c * FFN_CHUNK)
        act = gate * (1.0 / (1.0 + jnp.exp(-gate))) * a
        act_ref[:, c * FFN_CHUNK:(c + 1) * FFN_CHUNK] = act.astype(BF16)

    y = x + _dot(act_ref[...], wd_ref[...])
    if final_norm:
        y = _rms_norm(y, gf_ref[...])
    o_ref[...] = y


def conv_ffn(x, g, w_up, conv_w, conv_b, w_down, *, seq_len, tm, final_g=None):
    T, D = x.shape
    d_ff = w_down.shape[0]
    resident = dict(pipeline_mode=pl.Buffered(1))
    in_specs = [
        pl.BlockSpec((tm, D), lambda i: (i, 0)),
        pl.BlockSpec((1, D), lambda i: (0, 0)),
        pl.BlockSpec((D, 2 * d_ff), lambda i: (0, 0), **resident),
        pl.BlockSpec((CONV_WIDTH, 2 * d_ff), lambda i: (0, 0)),
        pl.BlockSpec((1, 2 * d_ff), lambda i: (0, 0)),
        pl.BlockSpec((d_ff, D), lambda i: (0, 0), **resident),
    ]
    args = [x, g, w_up, conv_w, conv_b, w_down]
    if final_g is not None:
        in_specs.append(pl.BlockSpec((1, D), lambda i: (0, 0)))
        args.append(final_g)
    kern = functools.partial(_ffn_kernel, d_ff=d_ff, tiles_per_seq=seq_len // tm,
                             final_norm=final_g is not None)
    return pl.pallas_call(
        kern,
        grid=(T // tm,),
        in_specs=in_specs,
        out_specs=pl.BlockSpec((tm, D), lambda i: (i, 0)),
        out_shape=jax.ShapeDtypeStruct((T, D), F32),
        scratch_shapes=[pltpu.VMEM((tm, D), BF16),
                        pltpu.VMEM((tm, d_ff), BF16),
                        pltpu.VMEM((8, 2 * d_ff), F32)],
        compiler_params=_params("arbitrary"),
        name="conv_ffn_final" if final_g is not None else "conv_ffn",
    )(*args)


def _dilated_kernel(q_ref, k_ref, v_ref, kp_ref, vp_ref, bias_ref, o_ref, lse_ref,
                    kf_ref, vf_ref, *, rows):
    W = B_WINDOW_STEPS
    first_chunk = pl.program_id(2) == 0
    kf_ref[0:W] = kp_ref[0]
    kf_ref[W:] = k_ref[0]
    vf_ref[0:W] = vp_ref[0]
    vf_ref[W:] = v_ref[0]
    lane = lax.broadcasted_iota(jnp.int32, (W, LANES), 1)
    col = lax.broadcasted_iota(jnp.int32, (W, 2 * W), 1)
    npair = q_ref.shape[2] // LANES

    def tile(n, _):
        r0 = pl.multiple_of(n * W, W)
        no_prev = jnp.logical_and(first_chunk, n == 0)
        kill_prev = jnp.logical_and(no_prev, col < W)
        for hp in range(npair):
            ls = slice(hp * LANES, (hp + 1) * LANES)
            q = q_ref[0, pl.ds(r0, W), ls]
            kk = kf_ref[pl.ds(r0, 2 * W), ls]
            vv = vf_ref[pl.ds(r0, 2 * W), ls]
            zero = jnp.zeros_like(q)
            o_h, lse_h = [], []
            for hh in range(2):
                in_head = (lane < HEAD_DIM) if hh == 0 else (lane >= HEAD_DIM)
                qm = jnp.where(in_head, q, zero)
                bias = jnp.where(kill_prev, NEG_BIG, bias_ref[2 * hp + hh])
                s = _dot_nt(qm, kk) + bias
                m = jnp.max(s, axis=-1, keepdims=True)
                p = jnp.exp(s - m)
                l = jnp.sum(p, axis=-1, keepdims=True)
                o_h.append(_dot(p.astype(BF16), vv) / l)
                lse_h.append(m + jnp.log(l))
            o_ref[0, pl.ds(r0, W), ls] = jnp.where(lane < HEAD_DIM, o_h[0], o_h[1]
                                                   ).astype(o_ref.dtype)
            lse_ref[0, pl.ds(r0, W), ls] = jnp.where(lane < HEAD_DIM, lse_h[0], lse_h[1])
        return 0

    lax.fori_loop(0, rows // W, tile, 0)


def dilated_attention(q, kv, bias, *, group, dil, rows):
    B, S, qw = q.shape
    n_groups = len(B_GROUP_DILATIONS)
    gw = qw // n_groups
    L = S // dil
    W = B_WINDOW_STEPS
    qv = q.reshape(B, L, dil * qw)
    kvv = kv.reshape(B, L, dil * 2 * qw)
    tiles = rows // W

    def prev_tile(c):
        return jnp.maximum(c * tiles - 1, 0)

    o, lse = pl.pallas_call(
        functools.partial(_dilated_kernel, rows=rows),
        grid=(B, dil, L // rows),
        in_specs=[
            pl.BlockSpec((1, rows, gw), lambda b, r, c: (b, c, r * n_groups + group)),
            pl.BlockSpec((1, rows, gw), lambda b, r, c: (b, c, r * 2 * n_groups + group)),
            pl.BlockSpec((1, rows, gw),
                         lambda b, r, c: (b, c, r * 2 * n_groups + n_groups + group)),
            pl.BlockSpec((1, W, gw),
                         lambda b, r, c: (b, prev_tile(c), r * 2 * n_groups + group)),
            pl.BlockSpec((1, W, gw),
                         lambda b, r, c: (b, prev_tile(c), r * 2 * n_groups + n_groups + group)),
            pl.BlockSpec(bias.shape, lambda b, r, c: (0, 0, 0)),
        ],
        out_specs=[pl.BlockSpec((1, rows, gw), lambda b, r, c: (b, c, r)),
                   pl.BlockSpec((1, rows, gw), lambda b, r, c: (b, c, r))],
        out_shape=[jax.ShapeDtypeStruct((B, L, dil * gw), F32),
                   jax.ShapeDtypeStruct((B, L, dil * gw), F32)],
        scratch_shapes=[pltpu.VMEM((rows + W, gw), BF16),
                        pltpu.VMEM((rows + W, gw), BF16)],
        compiler_params=_params("parallel", "parallel", "arbitrary"),
        name=f"dilated_attention_d{dil}",
    )(qv, kvv, kvv, kvv, kvv, bias)
    return o.reshape(B, S, gw), lse.reshape(B, S, gw)


def _dilated_bias(group, dil):
    W = B_WINDOW_STEPS
    n_heads = len(B_GROUP_DILATIONS) * B_HEADS_PER_GROUP
    idx = jnp.arange(1, n_heads + 1, dtype=F32)
    slopes = jnp.exp2(-8.0 * idx / n_heads)[group * B_HEADS_PER_GROUP:
                                             (group + 1) * B_HEADS_PER_GROUP]
    dist = jnp.arange(W)[:, None] + W - jnp.arange(2 * W)[None, :]
    band = (dist >= 0) & (dist <= W)
    bias = -slopes[:, None, None] * (dil * dist).astype(F32)
    return jnp.where(band[None], bias, NEG_BIG)


def _merge_proj_kernel(o0_ref, o1_ref, o2_ref, l0_ref, l1_ref, l2_ref, w_ref, x_ref, out_ref):
    l0, l1, l2 = l0_ref[...], l1_ref[...], l2_ref[...]
    m = jnp.maximum(jnp.maximum(l0, l1), l2)
    e0, e1, e2 = jnp.exp(l0 - m), jnp.exp(l1 - m), jnp.exp(l2 - m)
    o = (e0 * o0_ref[...] + e1 * o1_ref[...] + e2 * o2_ref[...]) / (e0 + e1 + e2)
    out_ref[...] = x_ref[...] + _dot(o.astype(BF16), w_ref[...])


def merge_proj_residual(outs, lses, w, x, *, tm):
    T, D = x.shape
    gw = w.shape[0]
    row_spec = pl.BlockSpec((tm, gw), lambda i: (i, 0))
    return pl.pallas_call(
        _merge_proj_kernel,
        grid=(T // tm,),
        in_specs=[row_spec] * 6 + [pl.BlockSpec((gw, D), lambda i: (0, 0)),
                                   pl.BlockSpec((tm, D), lambda i: (i, 0))],
        out_specs=pl.BlockSpec((tm, D), lambda i: (i, 0)),
        out_shape=jax.ShapeDtypeStruct((T, D), F32),
        compiler_params=_params("parallel"),
        name="merge_proj_residual",
    )(*outs, *lses, w, x)


def _tiles(T, S):
    return dict(
        proj_tm=min(1024, T), proj_tn=512,
        fox_tq=min(512, S),
        row_tm=min(512, S),
        dil_rows=1024,
    )


def kernel(x, a_w_in, a_b_f, a_w_out, b_w_q, b_w_out, kv_norm_g, w_kv, mix_norm_g,
           ffn_norm_g, ffn_w_up, ffn_conv_w, ffn_conv_b, ffn_w_down, final_norm_g):
    B, S, D = x.shape
    T = B * S
    t = _tiles(T, S)
    depth = mix_norm_g.shape[0]
    n_a = a_w_in.shape[0]
    qk_scale = HEAD_DIM ** -0.5
    row = lambda v: v.reshape(1, -1).astype(F32)

    xs = x.reshape(T, D)
    kv = None
    for layer in range(depth):
        if layer < n_a:
            qkv_w = A_HEADS * HEAD_DIM * 3
            w_qkv = a_w_in[layer][:, :qkv_w].astype(BF16)
            w_f = jnp.pad(a_w_in[layer][:, qkv_w:], ((0, 0), (0, LANES - A_HEADS))).astype(BF16)
            b_f = jnp.pad(a_b_f[layer].astype(F32), (0, LANES - A_HEADS)).reshape(1, LANES)
            qkv, logf = norm_proj(xs, row(mix_norm_g[layer]), w_qkv,
                                  tm=t["proj_tm"], tn=t["proj_tn"],
                                  scaled_tiles=A_HEADS * HEAD_DIM // t["proj_tn"],
                                  scale=qk_scale, gate=(w_f, b_f))
            logf = logf[:, :A_HEADS].reshape(B, S, A_HEADS).transpose(0, 2, 1)
            c = cumsum_lanes(logf.reshape(B * A_HEADS, S)).reshape(B, A_HEADS // 2, 2, S)
            tq = t["fox_tq"]
            c_col = c.transpose(0, 1, 3, 2)
            c_row = c.reshape(B, A_HEADS // 2, 2, S // tq, tq).transpose(0, 1, 3, 2, 4)
            o = fox_attention(qkv.reshape(B, S, qkv_w), c_col, c_row, tq=tq)
            xs = proj_residual(o.reshape(T, -1), a_w_out[layer].astype(BF16), xs,
                               tm=t["row_tm"])
        else:
            bl = layer - n_a
            if kv is None:
                kv = norm_proj(xs, row(kv_norm_g), w_kv.astype(BF16),
                               tm=t["proj_tm"], tn=t["proj_tn"])
            n_q = b_w_q.shape[2]
            q = norm_proj(xs, row(mix_norm_g[layer]), b_w_q[bl].astype(BF16),
                          tm=t["proj_tm"], tn=t["proj_tn"],
                          scaled_tiles=n_q // t["proj_tn"], scale=qk_scale)
            outs, lses = [], []
            for g, dil in enumerate(B_GROUP_DILATIONS):
                o_g, lse_g = dilated_attention(
                    q.reshape(B, S, n_q), kv.reshape(B, S, 2 * n_q), _dilated_bias(g, dil),
                    group=g, dil=dil, rows=min(t["dil_rows"], S // dil))
                outs.append(o_g.reshape(T, -1))
                lses.append(lse_g.reshape(T, -1))
            xs = merge_proj_residual(outs, lses, b_w_out[bl].astype(BF16), xs, tm=t["row_tm"])
        last = layer == depth - 1
        xs = conv_ffn(xs, row(ffn_norm_g[layer]), ffn_w_up[layer].astype(BF16),
                      ffn_conv_w[layer].astype(F32), row(ffn_conv_b[layer]),
                      ffn_w_down[layer].astype(BF16), seq_len=S, tm=t["row_tm"],
                      final_g=row(final_norm_g) if last else None)
    return xs.reshape(B, S, D)
```

```python
import functools

import jax
import jax.numpy as jnp
from jax import lax
from jax.experimental import pallas as pl
from jax.experimental.pallas import tpu as pltpu

F32 = jnp.float32
BF16 = jnp.bfloat16

RMS_EPS = 1e-6
HEAD_DIM = 64
LANES = 128
SUBLANES = 8
A_HEADS = 16
B_GROUP_DILATIONS = (1, 4, 16)
B_HEADS_PER_GROUP = 8
B_WINDOW_STEPS = 128
CONV_WIDTH = 3
NEG_BIG = -1e30
VMEM_LIMIT_BYTES = 56 * 1024 * 1024


def _params(*semantics):
    return pltpu.CompilerParams(dimension_semantics=semantics,
                                vmem_limit_bytes=VMEM_LIMIT_BYTES)


def _rms_norm(x, g):
    y = x * lax.rsqrt(jnp.mean(x * x, axis=-1, keepdims=True) + RMS_EPS)
    return y * g


def _dot(a, b):
    return jnp.dot(a, b, preferred_element_type=F32)


def _dot_nt(a, b):
    return lax.dot_general(a, b, (((1,), (1,)), ((), ())), preferred_element_type=F32)


def _norm_proj_kernel(x_ref, g_ref, w_ref, o_ref, h_ref, *, scaled_tiles, scale):
    j = pl.program_id(1)

    @pl.when(j == 0)
    def _():
        h_ref[...] = _rms_norm(x_ref[...], g_ref[...]).astype(BF16)

    acc = _dot(h_ref[...], w_ref[...])
    if scaled_tiles:
        acc = acc * jnp.where(j < scaled_tiles, scale, 1.0).astype(F32)
    o_ref[...] = acc.astype(o_ref.dtype)


def norm_proj(x, g, w, *, tm, tn, scaled_tiles=0, scale=1.0):
    T, D = x.shape
    N = w.shape[1]
    kern = functools.partial(_norm_proj_kernel, scaled_tiles=scaled_tiles, scale=scale)
    return pl.pallas_call(
        kern,
        grid=(T // tm, N // tn),
        in_specs=[pl.BlockSpec((tm, D), lambda i, j: (i, 0)),
                  pl.BlockSpec((1, D), lambda i, j: (0, 0)),
                  pl.BlockSpec((D, tn), lambda i, j: (0, j))],
        out_specs=pl.BlockSpec((tm, tn), lambda i, j: (i, j)),
        out_shape=jax.ShapeDtypeStruct((T, N), BF16),
        scratch_shapes=[pltpu.VMEM((tm, D), BF16)],
        compiler_params=_params("parallel", "arbitrary"),
        name="norm_proj",
    )(x, g, w)


def _fox_proj_kernel(x_ref, g_ref, wqk_ref, wvt_ref, wf_ref, bf_ref, qk_ref, vt_ref, logf_ref,
                     *, q_cols, scale):
    h = _rms_norm(x_ref[...], g_ref[...]).astype(BF16)
    z = _dot(h, wf_ref[...]) + bf_ref[...]
    logf_ref[...] = jnp.minimum(z, 0.0) - jnp.log(1.0 + jnp.exp(-jnp.abs(z)))
    qk_ref[:, :q_cols] = (_dot(h, wqk_ref[:, :q_cols]) * scale).astype(BF16)
    qk_ref[:, q_cols:] = _dot(h, wqk_ref[:, q_cols:]).astype(BF16)
    vt_ref[...] = _dot_nt(wvt_ref[...], h).astype(BF16)


def fox_proj(x, g, w_qk, w_vt, w_f, b_f, *, tm, q_cols, scale):
    T, D = x.shape
    n_qk = w_qk.shape[1]
    n_v = w_vt.shape[0]
    resident = dict(pipeline_mode=pl.Buffered(1))
    return pl.pallas_call(
        functools.partial(_fox_proj_kernel, q_cols=q_cols, scale=scale),
        grid=(T // tm,),
        in_specs=[pl.BlockSpec((tm, D), lambda i: (i, 0)),
                  pl.BlockSpec((1, D), lambda i: (0, 0)),
                  pl.BlockSpec((D, n_qk), lambda i: (0, 0), **resident),
                  pl.BlockSpec((n_v, D), lambda i: (0, 0), **resident),
                  pl.BlockSpec((D, LANES), lambda i: (0, 0)),
                  pl.BlockSpec((1, LANES), lambda i: (0, 0))],
        out_specs=[pl.BlockSpec((tm, n_qk), lambda i: (i, 0)),
                   pl.BlockSpec((n_v, tm), lambda i: (0, i)),
                   pl.BlockSpec((tm, LANES), lambda i: (i, 0))],
        out_shape=[jax.ShapeDtypeStruct((T, n_qk), BF16),
                   jax.ShapeDtypeStruct((n_v, T), BF16),
                   jax.ShapeDtypeStruct((T, LANES), F32)],
        compiler_params=_params("parallel"),
        name="fox_proj",
    )(x, g, w_qk, w_vt, w_f, b_f)


def _cumsum_kernel(x_ref, hi_ref, mid_ref, lo_ref):
    rows, n = x_ref.shape
    r = lax.broadcasted_iota(jnp.int32, (LANES, LANES), 0)
    c = lax.broadcasted_iota(jnp.int32, (LANES, LANES), 1)
    upper = jnp.where(r <= c, 1.0, 0.0).astype(F32)
    carry = jnp.zeros((rows, 1), F32)
    for k in range(n // LANES):
        sl = slice(k * LANES, (k + 1) * LANES)
        cs = jnp.dot(x_ref[:, sl], upper, preferred_element_type=F32,
                     precision=lax.Precision.HIGHEST)
        tot = cs + carry
        carry = carry + cs[:, LANES - 1:LANES]
        hi = tot.astype(BF16)
        rem = tot - hi.astype(F32)
        mid = rem.astype(BF16)
        hi_ref[:, sl] = hi
        mid_ref[:, sl] = mid
        lo_ref[:, sl] = (rem - mid.astype(F32)).astype(BF16)


def cumsum_lanes_split(x):
    out = jax.ShapeDtypeStruct(x.shape, BF16)
    return pl.pallas_call(
        _cumsum_kernel,
        out_shape=[out, out, out],
        compiler_params=pltpu.CompilerParams(vmem_limit_bytes=VMEM_LIMIT_BYTES),
        name="cumsum",
    )(x)


FOX_TQ = 512
FOX_TK = 256


def _row_groups(x):
    return [x[r * SUBLANES:(r + 1) * SUBLANES] for r in range(x.shape[0] // SUBLANES)]


def _tree(op, xs):
    while len(xs) > 1:
        xs = [op(xs[a], xs[a + 1]) for a in range(0, len(xs) - 1, 2)] + (
            [xs[-1]] if len(xs) % 2 else [])
    return xs[0]


def _sublane_allreduce(op, x):
    for shift in (4, 2, 1):
        x = op(x, pltpu.roll(x, shift, axis=0))
    return x


def _tile_rows(x, reps):
    return jnp.concatenate([x] * reps, axis=0)


def _fox_kernel(q_ref, k_ref, vt_ref, qaug_ref, kaug_ref, o_ref,
                qa_ref, ka_ref, m_ref, l_ref, acc_ref, st_ref):
    S = q_ref.shape[1]
    tq, tk = FOX_TQ, FOX_TK
    lane = lax.broadcasted_iota(jnp.int32, (S, LANES), 1)
    for hh in range(2):
        in_head = (lane < HEAD_DIM) if hh == 0 else (lane >= HEAD_DIM)
        qa_ref[hh] = jnp.where(in_head, q_ref[0], qaug_ref[0, 0])
        ka_ref[hh] = jnp.where(in_head, k_ref[0], kaug_ref[0, 0])
    key = lax.broadcasted_iota(jnp.int32, (tk, tq), 0)
    qry = lax.broadcasted_iota(jnp.int32, (tk, tq), 1)

    def scores(i, j):
        q0 = pl.multiple_of(i * tq, tq)
        k0 = pl.multiple_of(j * tk, tk)
        return [_dot_nt(ka_ref[hh, pl.ds(k0, tk), :], qa_ref[hh, pl.ds(q0, tq), :])
                for hh in range(2)]

    def step(i, j, diag, prefetch):
        nxt = scores(i, j + 1) if prefetch else None
        k0 = pl.multiple_of(j * tk, tk)
        pv = []
        for hh in range(2):
            st = st_ref[hh]
            if diag is not None:
                st = jnp.where(key + diag * tk <= qry, st, NEG_BIG)
            vt = vt_ref[hh * HEAD_DIM:(hh + 1) * HEAD_DIM, pl.ds(k0, tk)]
            m_old = m_ref[hh]
            m_new = jnp.maximum(m_old, _sublane_allreduce(
                jnp.maximum, _tree(jnp.maximum, _row_groups(st))))
            alpha = jnp.exp(m_old - m_new)
            p = jnp.exp(st - _tile_rows(m_new, tk // SUBLANES))
            m_ref[hh] = m_new
            l_ref[hh] = alpha * l_ref[hh] + _tree(jnp.add, _row_groups(p))
            pv.append((alpha, _dot(vt, p.astype(BF16))))
        if prefetch:
            for hh in range(2):
                st_ref[hh] = nxt[hh]
        for hh in range(2):
            alpha, o = pv[hh]
            acc_ref[hh] = acc_ref[hh] * _tile_rows(alpha, HEAD_DIM // SUBLANES) + o

    def kv_body(i, j, carry):
        step(i, j, None, True)
        return carry

    def q_tile(i, carry):
        m_ref[...] = jnp.full(m_ref.shape, NEG_BIG, F32)
        l_ref[...] = jnp.zeros(l_ref.shape, F32)
        acc_ref[...] = jnp.zeros(acc_ref.shape, F32)
        per_q = tq // tk
        first = scores(i, 0)
        for hh in range(2):
            st_ref[hh] = first[hh]
        lax.fori_loop(0, i * per_q, functools.partial(kv_body, i), 0)
        for dj in range(per_q):
            step(i, i * per_q + dj, dj, dj + 1 < per_q)
        outs = []
        for hh in range(2):
            l = _sublane_allreduce(jnp.add, l_ref[hh])
            outs.append(acc_ref[hh] / _tile_rows(l, HEAD_DIM // SUBLANES))
        ot = jnp.concatenate(outs, axis=0)
        o_ref[0, pl.ds(pl.multiple_of(i * tq, tq), tq), :] = ot.T.astype(o_ref.dtype)
        return carry

    lax.fori_loop(0, S // tq, q_tile, 0)


def fox_attention(qk, vt, qaug, kaug):
    B, S, W2 = qk.shape
    npair = W2 // 2 // LANES
    t = FOX_TQ
    return pl.pallas_call(
        _fox_kernel,
        grid=(B, npair),
        in_specs=[
            pl.BlockSpec((1, S, LANES), lambda b, h: (b, 0, h)),
            pl.BlockSpec((1, S, LANES), lambda b, h: (b, 0, npair + h)),
            pl.BlockSpec((LANES, S), lambda b, h: (h, b)),
            pl.BlockSpec((1, 1, S, LANES), lambda b, h: (b, h, 0, 0)),
            pl.BlockSpec((1, 1, S, LANES), lambda b, h: (b, h, 0, 0)),
        ],
        out_specs=pl.BlockSpec((1, S, LANES), lambda b, h: (b, 0, h)),
        out_shape=jax.ShapeDtypeStruct((B, S, npair * LANES), BF16),
        scratch_shapes=[pltpu.VMEM((2, S, LANES), BF16),
                        pltpu.VMEM((2, S, LANES), BF16),
                        pltpu.VMEM((2, SUBLANES, t), F32),
                        pltpu.VMEM((2, SUBLANES, t), F32),
                        pltpu.VMEM((2, HEAD_DIM, t), F32),
                        pltpu.VMEM((2, FOX_TK, t), F32)],
        compiler_params=_params("parallel", "parallel"),
        name="fox_attention",
    )(qk, qk, vt, qaug, kaug)


def _fox_aug(c_hi, c_mid, c_lo, B, S):
    H = c_hi.shape[0] // B
    c3 = jnp.stack([c_hi, c_mid, c_lo], axis=-1).reshape(B, H // 2, 2, S, 3)
    ones = jnp.ones_like(c3)
    pad = jnp.zeros(c3.shape[:-1] + (HEAD_DIM - 6,), BF16)
    qa = jnp.concatenate([-ones, c3, pad], axis=-1)
    ka = jnp.concatenate([c3, ones, pad], axis=-1)
    swap = lambda a: jnp.concatenate([a[:, :, 1], a[:, :, 0]], axis=-1)
    return swap(qa), swap(ka)


def _proj_residual_kernel(a_ref, w_ref, x_ref, o_ref):
    o_ref[...] = x_ref[...] + _dot(a_ref[...], w_ref[...])


def proj_residual(a, w, x, *, tm):
    T, K = a.shape
    D = w.shape[1]
    return pl.pallas_call(
        _proj_residual_kernel,
        grid=(T // tm,),
        in_specs=[pl.BlockSpec((tm, K), lambda i: (i, 0)),
                  pl.BlockSpec((K, D), lambda i: (0, 0)),
                  pl.BlockSpec((tm, D), lambda i: (i, 0))],
        out_specs=pl.BlockSpec((tm, D), lambda i: (i, 0)),
        out_shape=jax.ShapeDtypeStruct((T, D), F32),
        compiler_params=_params("parallel"),
        name="proj_residual",
    )(a, w, x)


FFN_CHUNK = 256


def _shift_rows(u, prev, shift):
    rolled = pltpu.roll(u, shift, axis=0)
    head = jnp.concatenate([prev, u[:SUBLANES]], axis=0)
    top = pltpu.roll(head, shift, axis=0)[SUBLANES:]
    return jnp.concatenate([top, rolled[SUBLANES:]], axis=0)


def _ffn_kernel(x_ref, g_ref, wup_ref, cw_ref, cb_ref, wd_ref, *rest,
                d_ff, tiles_per_seq, final_norm):
    if final_norm:
        gf_ref, o_ref, h_ref, act_ref, carry_ref = rest
    else:
        o_ref, h_ref, act_ref, carry_ref = rest
    tm = x_ref.shape[0]
    first = (pl.program_id(0) % tiles_per_seq) == 0

    @pl.when(pl.program_id(0) == 0)
    def _():
        carry_ref[...] = jnp.zeros_like(carry_ref)

    x = x_ref[...]
    h_ref[...] = _rms_norm(x, g_ref[...]).astype(BF16)

    def conv_branch(col):
        sl = slice(col, col + FFN_CHUNK)
        u = _dot(h_ref[...], wup_ref[:, sl])
        prev = jnp.where(first, 0.0, carry_ref[:, sl])
        carry_ref[:, sl] = u[tm - SUBLANES:]
        u1 = _shift_rows(u, prev, 1)
        u2 = _shift_rows(u, prev, 2)
        return (cw_ref[0:1, sl] * u2 + cw_ref[1:2, sl] * u1 + cw_ref[2:3, sl] * u
                + cb_ref[:, sl])

    for c in range(d_ff // FFN_CHUNK):
        a = conv_branch(c * FFN_CHUNK)
        gate = conv_branch(d_ff + c * FFN_CHUNK)
        act = gate * (1.0 / (1.0 + jnp.exp(-gate))) * a
        act_ref[:, c * FFN_CHUNK:(c + 1) * FFN_CHUNK] = act.astype(BF16)

    y = x + _dot(act_ref[...], wd_ref[...])
    if final_norm:
        y = _rms_norm(y, gf_ref[...])
    o_ref[...] = y


def conv_ffn(x, g, w_up, conv_w, conv_b, w_down, *, seq_len, tm, final_g=None):
    T, D = x.shape
    d_ff = w_down.shape[0]
    resident = dict(pipeline_mode=pl.Buffered(1))
    in_specs = [
        pl.BlockSpec((tm, D), lambda i: (i, 0)),
        pl.BlockSpec((1, D), lambda i: (0, 0)),
        pl.BlockSpec((D, 2 * d_ff), lambda i: (0, 0), **resident),
        pl.BlockSpec((CONV_WIDTH, 2 * d_ff), lambda i: (0, 0)),
        pl.BlockSpec((1, 2 * d_ff), lambda i: (0, 0)),
        pl.BlockSpec((d_ff, D), lambda i: (0, 0), **resident),
    ]
    args = [x, g, w_up, conv_w, conv_b, w_down]
    if final_g is not None:
        in_specs.append(pl.BlockSpec((1, D), lambda i: (0, 0)))
        args.append(final_g)
    kern = functools.partial(_ffn_kernel, d_ff=d_ff, tiles_per_seq=seq_len // tm,
                             final_norm=final_g is not None)
    return pl.pallas_call(
        kern,
        grid=(T // tm,),
        in_specs=in_specs,
        out_specs=pl.BlockSpec((tm, D), lambda i: (i, 0)),
        out_shape=jax.ShapeDtypeStruct((T, D), F32),
        scratch_shapes=[pltpu.VMEM((tm, D), BF16),
                        pltpu.VMEM((tm, d_ff), BF16),
                        pltpu.VMEM((SUBLANES, 2 * d_ff), F32)],
        compiler_params=_params("arbitrary"),
        name="conv_ffn_final" if final_g is not None else "conv_ffn",
    )(*args)


def _dilated_kernel(q_ref, k_ref, v_ref, kp_ref, vp_ref, bias_ref, o_ref, lse_ref,
                    kf_ref, vf_ref, *, rows):
    W = B_WINDOW_STEPS
    first_chunk = pl.program_id(2) == 0
    kf_ref[0:W] = kp_ref[0]
    kf_ref[W:] = k_ref[0]
    vf_ref[0:W] = vp_ref[0]
    vf_ref[W:] = v_ref[0]
    lane = lax.broadcasted_iota(jnp.int32, (W, LANES), 1)
    col = lax.broadcasted_iota(jnp.int32, (W, 2 * W), 1)
    npair = q_ref.shape[2] // LANES

    def tile(n, _):
        r0 = pl.multiple_of(n * W, W)
        no_prev = jnp.logical_and(first_chunk, n == 0)
        kill_prev = jnp.logical_and(no_prev, col < W)
        for hp in range(npair):
            ls = slice(hp * LANES, (hp + 1) * LANES)
            q = q_ref[0, pl.ds(r0, W), ls]
            kk = kf_ref[pl.ds(r0, 2 * W), ls]
            vv = vf_ref[pl.ds(r0, 2 * W), ls]
            zero = jnp.zeros_like(q)
            o_h, lse_h = [], []
            for hh in range(2):
                in_head = (lane < HEAD_DIM) if hh == 0 else (lane >= HEAD_DIM)
                qm = jnp.where(in_head, q, zero)
                bias = jnp.where(kill_prev, NEG_BIG, bias_ref[2 * hp + hh])
                s = _dot_nt(qm, kk) + bias
                m = jnp.max(s, axis=-1, keepdims=True)
                p = jnp.exp(s - m)
                l = jnp.sum(p, axis=-1, keepdims=True)
                o_h.append(_dot(p.astype(BF16), vv) / l)
                lse_h.append(m + jnp.log(l))
            o_ref[0, pl.ds(r0, W), ls] = jnp.where(lane < HEAD_DIM, o_h[0], o_h[1]
                                                   ).astype(o_ref.dtype)
            lse_ref[0, pl.ds(r0, W), ls] = jnp.where(lane < HEAD_DIM, lse_h[0], lse_h[1])
        return 0

    lax.fori_loop(0, rows // W, tile, 0)


def dilated_attention(q, kv, bias, *, group, dil, rows):
    B, S, qw = q.shape
    n_groups = len(B_GROUP_DILATIONS)
    gw = qw // n_groups
    L = S // dil
    W = B_WINDOW_STEPS
    qv = q.reshape(B, L, dil * qw)
    kvv = kv.reshape(B, L, dil * 2 * qw)
    tiles = rows // W

    def prev_tile(c):
        return jnp.maximum(c * tiles - 1, 0)

    o, lse = pl.pallas_call(
        functools.partial(_dilated_kernel, rows=rows),
        grid=(B, dil, L // rows),
        in_specs=[
            pl.BlockSpec((1, rows, gw), lambda b, r, c: (b, c, r * n_groups + group)),
            pl.BlockSpec((1, rows, gw), lambda b, r, c: (b, c, r * 2 * n_groups + group)),
            pl.BlockSpec((1, rows, gw),
                         lambda b, r, c: (b, c, r * 2 * n_groups + n_groups + group)),
            pl.BlockSpec((1, W, gw),
                         lambda b, r, c: (b, prev_tile(c), r * 2 * n_groups + group)),
            pl.BlockSpec((1, W, gw),
                         lambda b, r, c: (b, prev_tile(c), r * 2 * n_groups + n_groups + group)),
            pl.BlockSpec(bias.shape, lambda b, r, c: (0, 0, 0)),
        ],
        out_specs=[pl.BlockSpec((1, rows, gw), lambda b, r, c: (b, c, r)),
                   pl.BlockSpec((1, rows, gw), lambda b, r, c: (b, c, r))],
        out_shape=[jax.ShapeDtypeStruct((B, L, dil * gw), F32),
                   jax.ShapeDtypeStruct((B, L, dil * gw), F32)],
        scratch_shapes=[pltpu.VMEM((rows + W, gw), BF16),
                        pltpu.VMEM((rows + W, gw), BF16)],
        compiler_params=_params("parallel", "parallel", "arbitrary"),
        name=f"dilated_attention_d{dil}",
    )(qv, kvv, kvv, kvv, kvv, bias)
    return o.reshape(B, S, gw), lse.reshape(B, S, gw)


def _dilated_bias(group, dil):
    W = B_WINDOW_STEPS
    n_heads = len(B_GROUP_DILATIONS) * B_HEADS_PER_GROUP
    idx = jnp.arange(1, n_heads + 1, dtype=F32)
    slopes = jnp.exp2(-8.0 * idx / n_heads)[group * B_HEADS_PER_GROUP:
                                             (group + 1) * B_HEADS_PER_GROUP]
    dist = jnp.arange(W)[:, None] + W - jnp.arange(2 * W)[None, :]
    band = (dist >= 0) & (dist <= W)
    bias = -slopes[:, None, None] * (dil * dist).astype(F32)
    return jnp.where(band[None], bias, NEG_BIG)


def _merge_proj_kernel(o0_ref, o1_ref, o2_ref, l0_ref, l1_ref, l2_ref, w_ref, x_ref, out_ref):
    l0, l1, l2 = l0_ref[...], l1_ref[...], l2_ref[...]
    m = jnp.maximum(jnp.maximum(l0, l1), l2)
    e0, e1, e2 = jnp.exp(l0 - m), jnp.exp(l1 - m), jnp.exp(l2 - m)
    o = (e0 * o0_ref[...] + e1 * o1_ref[...] + e2 * o2_ref[...]) / (e0 + e1 + e2)
    out_ref[...] = x_ref[...] + _dot(o.astype(BF16), w_ref[...])


def merge_proj_residual(outs, lses, w, x, *, tm):
    T, D = x.shape
    gw = w.shape[0]
    row_spec = pl.BlockSpec((tm, gw), lambda i: (i, 0))
    return pl.pallas_call(
        _merge_proj_kernel,
        grid=(T // tm,),
        in_specs=[row_spec] * 6 + [pl.BlockSpec((gw, D), lambda i: (0, 0)),
                                   pl.BlockSpec((tm, D), lambda i: (i, 0))],
        out_specs=pl.BlockSpec((tm, D), lambda i: (i, 0)),
        out_shape=jax.ShapeDtypeStruct((T, D), F32),
        compiler_params=_params("parallel"),
        name="merge_proj_residual",
    )(*outs, *lses, w, x)


def _tiles(T, S):
    return dict(
        proj_tm=min(1024, T), proj_tn=512,
        row_tm=min(512, S),
        dil_rows=1024,
    )


def kernel(x, a_w_in, a_b_f, a_w_out, b_w_q, b_w_out, kv_norm_g, w_kv, mix_norm_g,
           ffn_norm_g, ffn_w_up, ffn_conv_w, ffn_conv_b, ffn_w_down, final_norm_g):
    B, S, D = x.shape
    T = B * S
    t = _tiles(T, S)
    depth = mix_norm_g.shape[0]
    n_a = a_w_in.shape[0]
    qk_scale = HEAD_DIM ** -0.5
    row = lambda v: v.reshape(1, -1).astype(F32)

    xs = x.reshape(T, D)
    kv = None
    for layer in range(depth):
        if layer < n_a:
            hw = A_HEADS * HEAD_DIM
            w_in = a_w_in[layer]
            w_qk = w_in[:, :2 * hw].astype(BF16)
            w_vt = w_in[:, 2 * hw:3 * hw].T.astype(BF16)
            w_f = jnp.pad(w_in[:, 3 * hw:], ((0, 0), (0, LANES - A_HEADS))).astype(BF16)
            b_f = jnp.pad(a_b_f[layer].astype(F32), (0, LANES - A_HEADS)).reshape(1, LANES)
            qk, vt, logf = fox_proj(xs, row(mix_norm_g[layer]), w_qk, w_vt, w_f, b_f,
                                    tm=t["row_tm"], q_cols=hw, scale=qk_scale)
            logf = logf[:, :A_HEADS].reshape(B, S, A_HEADS).transpose(0, 2, 1)
            c_parts = cumsum_lanes_split(logf.reshape(B * A_HEADS, S))
            qaug, kaug = _fox_aug(*c_parts, B, S)
            o = fox_attention(qk.reshape(B, S, 2 * hw), vt, qaug, kaug)
            xs = proj_residual(o.reshape(T, -1), a_w_out[layer].astype(BF16), xs,
                               tm=t["row_tm"])
        else:
            bl = layer - n_a
            if kv is None:
                kv = norm_proj(xs, row(kv_norm_g), w_kv.astype(BF16),
                               tm=t["proj_tm"], tn=t["proj_tn"])
            n_q = b_w_q.shape[2]
            q = norm_proj(xs, row(mix_norm_g[layer]), b_w_q[bl].astype(BF16),
                          tm=t["proj_tm"], tn=t["proj_tn"],
                          scaled_tiles=n_q // t["proj_tn"], scale=qk_scale)
            outs, lses = [], []
            for g, dil in enumerate(B_GROUP_DILATIONS):
                o_g, lse_g = dilated_attention(
                    q.reshape(B, S, n_q), kv.reshape(B, S, 2 * n_q), _dilated_bias(g, dil),
                    group=g, dil=dil, rows=min(t["dil_rows"], S // dil))
                outs.append(o_g.reshape(T, -1))
                lses.append(lse_g.reshape(T, -1))
            xs = merge_proj_residual(outs, lses, b_w_out[bl].astype(BF16), xs, tm=t["row_tm"])
        last = layer == depth - 1
        xs = conv_ffn(xs, row(ffn_norm_g[layer]), ffn_w_up[layer].astype(BF16),
                      ffn_conv_w[layer].astype(F32), row(ffn_conv_b[layer]),
                      ffn_w_down[layer].astype(BF16), seq_len=S, tm=t["row_tm"],
                      final_g=row(final_norm_g) if last else None)
    return xs.reshape(B, S, D)
```

```python
import functools

import numpy as np
import jax
import jax.numpy as jnp
from jax import lax
from jax.experimental import pallas as pl
from jax.experimental.pallas import tpu as pltpu

F32 = jnp.float32
BF16 = jnp.bfloat16

RMS_EPS = 1e-6
HEAD_DIM = 64
LANES = 128
SUBLANES = 8
A_HEADS = 16
B_GROUP_DILATIONS = (1, 4, 16)
B_HEADS_PER_GROUP = 8
B_WINDOW_STEPS = 128
CONV_WIDTH = 3
NEG_BIG = -1e30
VMEM_LIMIT_BYTES = 56 * 1024 * 1024


def _params(*semantics):
    return pltpu.CompilerParams(dimension_semantics=semantics,
                                vmem_limit_bytes=VMEM_LIMIT_BYTES)


def _rms_norm(x, g):
    y = x * lax.rsqrt(jnp.mean(x * x, axis=-1, keepdims=True) + RMS_EPS)
    return y * g


def _dot(a, b):
    return jnp.dot(a, b, preferred_element_type=F32)


def _dot_nt(a, b):
    return lax.dot_general(a, b, (((1,), (1,)), ((), ())), preferred_element_type=F32)


def _b_proj_kernel(x_ref, gq_ref, gkv_ref, wq_ref, wkv_ref, *rest, dils, scale):
    outs = rest[:len(dils)]
    hq_ref, hkv_ref = rest[len(dils):]
    tm, D = x_ref.shape
    gw = wq_ref.shape[1] // len(dils)
    n_slab = D // LANES
    x = x_ref[...]
    xn = x * lax.rsqrt(jnp.mean(x * x, axis=-1, keepdims=True) + RMS_EPS)
    hq = xn * gq_ref[...]
    hkv = xn * gkv_ref[...]
    for s in range(n_slab):
        hq_ref[s] = hq[:, s * LANES:(s + 1) * LANES]
        hkv_ref[s] = hkv[:, s * LANES:(s + 1) * LANES]

    def residue_major(h_ref, d):
        rows = tm // d
        return jnp.concatenate(
            [jnp.concatenate([h_ref[s, pl.ds(r, rows, stride=d), :] for s in range(n_slab)],
                             axis=1) for r in range(d)], axis=0).astype(BF16)

    for g, d in enumerate(dils):
        if d == 1:
            a_q, a_kv = hq.astype(BF16), hkv.astype(BF16)
        else:
            a_q, a_kv = residue_major(hq_ref, d), residue_major(hkv_ref, d)
        q = (_dot(a_q, wq_ref[:, g * gw:(g + 1) * gw]) * scale).astype(BF16)
        kv = _dot(a_kv, wkv_ref[:, g * 2 * gw:(g + 1) * 2 * gw]).astype(BF16)
        rows = tm // d
        for r in range(d):
            outs[g][0, r, :, :gw] = q[r * rows:(r + 1) * rows]
            outs[g][0, r, :, gw:] = kv[r * rows:(r + 1) * rows]


def b_proj(x, g_q, g_kv, w_q, w_kv, *, batch, tm, dils, scale):
    T, D = x.shape
    S = T // batch
    tps = S // tm
    gw = w_q.shape[1] // len(dils)
    resident = dict(pipeline_mode=pl.Buffered(1))
    return pl.pallas_call(
        functools.partial(_b_proj_kernel, dils=dils, scale=scale),
        grid=(T // tm,),
        in_specs=[pl.BlockSpec((tm, D), lambda i: (i, 0)),
                  pl.BlockSpec((1, D), lambda i: (0, 0)),
                  pl.BlockSpec((1, D), lambda i: (0, 0)),
                  pl.BlockSpec(w_q.shape, lambda i: (0, 0), **resident),
                  pl.BlockSpec(w_kv.shape, lambda i: (0, 0), **resident)],
        out_specs=[pl.BlockSpec((1, d, tm // d, 3 * gw), lambda i: (i // tps, 0, i % tps, 0))
                   for d in dils],
        out_shape=[jax.ShapeDtypeStruct((batch, d, S // d, 3 * gw), BF16) for d in dils],
        scratch_shapes=[pltpu.VMEM((D // LANES, tm, LANES), F32),
                        pltpu.VMEM((D // LANES, tm, LANES), F32)],
        compiler_params=_params("parallel"),
        name="b_proj",
    )(x, g_q, g_kv, w_q, w_kv)


def _split3(x):
    hi = x.astype(BF16)
    rem = x - hi.astype(F32)
    mid = rem.astype(BF16)
    return hi, mid, (rem - mid.astype(F32)).astype(BF16)


def _fox_proj_kernel(x_ref, g_ref, wqk_ref, wvt_ref, wf_ref, bf_ref, eq_ref, ek_ref, cq_ref,
                     ck_ref, qk_ref, vt_ref, qaug_ref, kaug_ref, carry_ref,
                     *, q_cols, scale, tiles_per_seq):
    tm = x_ref.shape[0]

    @pl.when(pl.program_id(0) == 0)
    def _():
        carry_ref[...] = jnp.zeros_like(carry_ref)

    h = _rms_norm(x_ref[...], g_ref[...]).astype(BF16)
    qk_ref[:, :q_cols] = (_dot(h, wqk_ref[:, :q_cols]) * scale).astype(BF16)
    qk_ref[:, q_cols:] = _dot(h, wqk_ref[:, q_cols:]).astype(BF16)
    vt_ref[...] = _dot_nt(wvt_ref[...], h).astype(BF16)

    z = _dot(h, wf_ref[...]) + bf_ref[...]
    logf = jnp.minimum(z, 0.0) - jnp.log(1.0 + jnp.exp(-jnp.abs(z)))
    r = lax.broadcasted_iota(jnp.int32, (tm, tm), 0)
    c = lax.broadcasted_iota(jnp.int32, (tm, tm), 1)
    tri = jnp.where(r >= c, 1.0, 0.0).astype(BF16)
    p_hi, p_mid, p_lo = _split3(logf)
    first = (pl.program_id(0) % tiles_per_seq) == 0
    prev = jnp.where(first, 0.0, carry_ref[SUBLANES - 1:SUBLANES, :])
    cum = _dot(tri, p_hi) + _dot(tri, p_mid) + _dot(tri, p_lo) + prev
    carry_ref[...] = cum[tm - SUBLANES:]
    c3 = jnp.concatenate(_split3(cum), axis=1)
    qaug_ref[...] = (_dot(c3, eq_ref[...]) + cq_ref[...]).astype(BF16)
    kaug_ref[...] = (_dot(c3, ek_ref[...]) + ck_ref[...]).astype(BF16)


def _aug_placement(n_heads):
    width = n_heads // 2 * LANES
    eq = np.zeros((3 * LANES, width), np.float32)
    ek = np.zeros((3 * LANES, width), np.float32)
    cq = np.zeros((1, width), np.float32)
    ck = np.zeros((1, width), np.float32)
    for hd in range(n_heads):
        base = hd // 2 * LANES + (HEAD_DIM if hd % 2 == 0 else 0)
        for part in range(3):
            ek[part * LANES + hd, base + part] = 1.0
            eq[part * LANES + hd, base + 3 + part] = 1.0
            cq[0, base + part] = -1.0
            ck[0, base + 3 + part] = 1.0
    return (jnp.asarray(eq, BF16), jnp.asarray(ek, BF16), jnp.asarray(cq), jnp.asarray(ck))


def fox_proj(x, g, w_qk, w_vt, w_f, b_f, *, tm, seq_len, n_heads, q_cols, scale):
    T, D = x.shape
    n_qk = w_qk.shape[1]
    n_v = w_vt.shape[0]
    aug_w = n_heads // 2 * LANES
    resident = dict(pipeline_mode=pl.Buffered(1))
    const = lambda shape: pl.BlockSpec(shape, lambda i: (0, 0))
    return pl.pallas_call(
        functools.partial(_fox_proj_kernel, q_cols=q_cols, scale=scale,
                          tiles_per_seq=seq_len // tm),
        grid=(T // tm,),
        in_specs=[pl.BlockSpec((tm, D), lambda i: (i, 0)),
                  const((1, D)),
                  pl.BlockSpec((D, n_qk), lambda i: (0, 0), **resident),
                  pl.BlockSpec((n_v, D), lambda i: (0, 0), **resident),
                  const((D, LANES)), const((1, LANES)),
                  const((3 * LANES, aug_w)), const((3 * LANES, aug_w)),
                  const((1, aug_w)), const((1, aug_w))],
        out_specs=[pl.BlockSpec((tm, n_qk), lambda i: (i, 0)),
                   pl.BlockSpec((n_v, tm), lambda i: (0, i)),
                   pl.BlockSpec((tm, aug_w), lambda i: (i, 0)),
                   pl.BlockSpec((tm, aug_w), lambda i: (i, 0))],
        out_shape=[jax.ShapeDtypeStruct((T, n_qk), BF16),
                   jax.ShapeDtypeStruct((n_v, T), BF16),
                   jax.ShapeDtypeStruct((T, aug_w), BF16),
                   jax.ShapeDtypeStruct((T, aug_w), BF16)],
        scratch_shapes=[pltpu.VMEM((SUBLANES, LANES), F32)],
        compiler_params=_params("arbitrary"),
        name="fox_proj",
    )(x, g, w_qk, w_vt, w_f, b_f, *_aug_placement(n_heads))


FOX_TQ = 512
FOX_TK = 256


def _row_groups(x):
    return [x[r * SUBLANES:(r + 1) * SUBLANES] for r in range(x.shape[0] // SUBLANES)]


def _tree(op, xs):
    while len(xs) > 1:
        xs = [op(xs[a], xs[a + 1]) for a in range(0, len(xs) - 1, 2)] + (
            [xs[-1]] if len(xs) % 2 else [])
    return xs[0]


def _sublane_allreduce(op, x):
    for shift in (4, 2, 1):
        x = op(x, pltpu.roll(x, shift, axis=0))
    return x


def _tile_rows(x, reps):
    return jnp.concatenate([x] * reps, axis=0)


def _fox_kernel(q_ref, k_ref, vt_ref, qaug_ref, kaug_ref, o_ref,
                qa_ref, ka_ref, m_ref, l_ref, acc_ref, st_ref):
    S = q_ref.shape[1]
    tq, tk = FOX_TQ, FOX_TK
    lane = lax.broadcasted_iota(jnp.int32, (S, LANES), 1)
    for hh in range(2):
        in_head = (lane < HEAD_DIM) if hh == 0 else (lane >= HEAD_DIM)
        qa_ref[hh] = jnp.where(in_head, q_ref[0], qaug_ref[0])
        ka_ref[hh] = jnp.where(in_head, k_ref[0], kaug_ref[0])
    key = lax.broadcasted_iota(jnp.int32, (tk, tq), 0)
    qry = lax.broadcasted_iota(jnp.int32, (tk, tq), 1)

    def scores(i, j):
        q0 = pl.multiple_of(i * tq, tq)
        k0 = pl.multiple_of(j * tk, tk)
        return [_dot_nt(ka_ref[hh, pl.ds(k0, tk), :], qa_ref[hh, pl.ds(q0, tq), :])
                for hh in range(2)]

    def step(i, j, diag, prefetch):
        nxt = scores(i, j + 1) if prefetch else None
        k0 = pl.multiple_of(j * tk, tk)
        pv = []
        for hh in range(2):
            st = st_ref[hh]
            if diag is not None:
                st = jnp.where(key + diag * tk <= qry, st, NEG_BIG)
            vt = vt_ref[hh * HEAD_DIM:(hh + 1) * HEAD_DIM, pl.ds(k0, tk)]
            m_old = m_ref[hh]
            m_new = jnp.maximum(m_old, _sublane_allreduce(
                jnp.maximum, _tree(jnp.maximum, _row_groups(st))))
            alpha = jnp.exp(m_old - m_new)
            p = jnp.exp(st - _tile_rows(m_new, tk // SUBLANES))
            m_ref[hh] = m_new
            l_ref[hh] = alpha * l_ref[hh] + _tree(jnp.add, _row_groups(p))
            pv.append((alpha, _dot(vt, p.astype(BF16))))
        if prefetch:
            for hh in range(2):
                st_ref[hh] = nxt[hh]
        for hh in range(2):
            alpha, o = pv[hh]
            acc_ref[hh] = acc_ref[hh] * _tile_rows(alpha, HEAD_DIM // SUBLANES) + o

    def kv_body(i, j, carry):
        step(i, j, None, True)
        return carry

    def q_tile(i, carry):
        m_ref[...] = jnp.full(m_ref.shape, NEG_BIG, F32)
        l_ref[...] = jnp.zeros(l_ref.shape, F32)
        acc_ref[...] = jnp.zeros(acc_ref.shape, F32)
        per_q = tq // tk
        first = scores(i, 0)
        for hh in range(2):
            st_ref[hh] = first[hh]
        lax.fori_loop(0, i * per_q, functools.partial(kv_body, i), 0)
        for dj in range(per_q):
            step(i, i * per_q + dj, dj, dj + 1 < per_q)
        outs = []
        for hh in range(2):
            l = _sublane_allreduce(jnp.add, l_ref[hh])
            outs.append(acc_ref[hh] / _tile_rows(l, HEAD_DIM // SUBLANES))
        ot = jnp.concatenate(outs, axis=0)
        o_ref[0, pl.ds(pl.multiple_of(i * tq, tq), tq), :] = ot.T.astype(o_ref.dtype)
        return carry

    lax.fori_loop(0, S // tq, q_tile, 0)


def fox_attention(qk, vt, qaug, kaug):
    B, S, W2 = qk.shape
    npair = W2 // 2 // LANES
    t = FOX_TQ
    return pl.pallas_call(
        _fox_kernel,
        grid=(B, npair),
        in_specs=[
            pl.BlockSpec((1, S, LANES), lambda b, h: (b, 0, h)),
            pl.BlockSpec((1, S, LANES), lambda b, h: (b, 0, npair + h)),
            pl.BlockSpec((LANES, S), lambda b, h: (h, b)),
            pl.BlockSpec((1, S, LANES), lambda b, h: (b, 0, h)),
            pl.BlockSpec((1, S, LANES), lambda b, h: (b, 0, h)),
        ],
        out_specs=pl.BlockSpec((1, S, LANES), lambda b, h: (b, 0, h)),
        out_shape=jax.ShapeDtypeStruct((B, S, npair * LANES), BF16),
        scratch_shapes=[pltpu.VMEM((2, S, LANES), BF16),
                        pltpu.VMEM((2, S, LANES), BF16),
                        pltpu.VMEM((2, SUBLANES, t), F32),
                        pltpu.VMEM((2, SUBLANES, t), F32),
                        pltpu.VMEM((2, HEAD_DIM, t), F32),
                        pltpu.VMEM((2, FOX_TK, t), F32)],
        compiler_params=_params("parallel", "parallel"),
        name="fox_attention",
    )(qk, qk, vt, qaug, kaug)


def _proj_residual_kernel(a_ref, w_ref, x_ref, o_ref):
    o_ref[...] = x_ref[...] + _dot(a_ref[...], w_ref[...])


def proj_residual(a, w, x, *, tm):
    T, K = a.shape
    D = w.shape[1]
    return pl.pallas_call(
        _proj_residual_kernel,
        grid=(T // tm,),
        in_specs=[pl.BlockSpec((tm, K), lambda i: (i, 0)),
                  pl.BlockSpec((K, D), lambda i: (0, 0)),
                  pl.BlockSpec((tm, D), lambda i: (i, 0))],
        out_specs=pl.BlockSpec((tm, D), lambda i: (i, 0)),
        out_shape=jax.ShapeDtypeStruct((T, D), F32),
        compiler_params=_params("parallel"),
        name="proj_residual",
    )(a, w, x)


FFN_CHUNK = 256


def _shift_rows(u, prev, shift):
    rolled = pltpu.roll(u, shift, axis=0)
    head = jnp.concatenate([prev, u[:SUBLANES]], axis=0)
    top = pltpu.roll(head, shift, axis=0)[SUBLANES:]
    return jnp.concatenate([top, rolled[SUBLANES:]], axis=0)


def _ffn_kernel(x_ref, g_ref, wup_ref, cw_ref, cb_ref, wd_ref, *rest,
                d_ff, tiles_per_seq, final_norm):
    if final_norm:
        gf_ref, o_ref, h_ref, act_ref, carry_ref = rest
    else:
        o_ref, h_ref, act_ref, carry_ref = rest
    tm = x_ref.shape[0]
    first = (pl.program_id(0) % tiles_per_seq) == 0

    @pl.when(pl.program_id(0) == 0)
    def _():
        carry_ref[...] = jnp.zeros_like(carry_ref)

    x = x_ref[...]
    h_ref[...] = _rms_norm(x, g_ref[...]).astype(BF16)

    def conv_branch(col):
        sl = slice(col, col + FFN_CHUNK)
        u = _dot(h_ref[...], wup_ref[:, sl])
        prev = jnp.where(first, 0.0, carry_ref[:, sl])
        carry_ref[:, sl] = u[tm - SUBLANES:]
        u1 = _shift_rows(u, prev, 1)
        u2 = _shift_rows(u, prev, 2)
        return (cw_ref[0:1, sl] * u2 + cw_ref[1:2, sl] * u1 + cw_ref[2:3, sl] * u
                + cb_ref[:, sl])

    for c in range(d_ff // FFN_CHUNK):
        a = conv_branch(c * FFN_CHUNK)
        gate = conv_branch(d_ff + c * FFN_CHUNK)
        act = gate * (1.0 / (1.0 + jnp.exp(-gate))) * a
        act_ref[:, c * FFN_CHUNK:(c + 1) * FFN_CHUNK] = act.astype(BF16)

    y = x + _dot(act_ref[...], wd_ref[...])
    if final_norm:
        y = _rms_norm(y, gf_ref[...])
    o_ref[...] = y


def conv_ffn(x, g, w_up, conv_w, conv_b, w_down, *, seq_len, tm, final_g=None):
    T, D = x.shape
    d_ff = w_down.shape[0]
    resident = dict(pipeline_mode=pl.Buffered(1))
    in_specs = [
        pl.BlockSpec((tm, D), lambda i: (i, 0)),
        pl.BlockSpec((1, D), lambda i: (0, 0)),
        pl.BlockSpec((D, 2 * d_ff), lambda i: (0, 0), **resident),
        pl.BlockSpec((CONV_WIDTH, 2 * d_ff), lambda i: (0, 0)),
        pl.BlockSpec((1, 2 * d_ff), lambda i: (0, 0)),
        pl.BlockSpec((d_ff, D), lambda i: (0, 0), **resident),
    ]
    args = [x, g, w_up, conv_w, conv_b, w_down]
    if final_g is not None:
        in_specs.append(pl.BlockSpec((1, D), lambda i: (0, 0)))
        args.append(final_g)
    kern = functools.partial(_ffn_kernel, d_ff=d_ff, tiles_per_seq=seq_len // tm,
                             final_norm=final_g is not None)
    return pl.pallas_call(
        kern,
        grid=(T // tm,),
        in_specs=in_specs,
        out_specs=pl.BlockSpec((tm, D), lambda i: (i, 0)),
        out_shape=jax.ShapeDtypeStruct((T, D), F32),
        scratch_shapes=[pltpu.VMEM((tm, D), BF16),
                        pltpu.VMEM((tm, d_ff), BF16),
                        pltpu.VMEM((SUBLANES, 2 * d_ff), F32)],
        compiler_params=_params("arbitrary"),
        name="conv_ffn_final" if final_g is not None else "conv_ffn",
    )(*args)


def _dilated_kernel(q_ref, k_ref, v_ref, kp_ref, vp_ref, bias_ref, o_ref, lse_ref,
                    kf_ref, vf_ref, *, dil, rows):
    W = B_WINDOW_STEPS
    first_chunk = pl.program_id(1) == 0
    kf_ref[:, 0:W] = kp_ref[0]
    kf_ref[:, W:] = k_ref[0]
    vf_ref[:, 0:W] = vp_ref[0]
    vf_ref[:, W:] = v_ref[0]
    lane = lax.broadcasted_iota(jnp.int32, (W, LANES), 1)
    col = lax.broadcasted_iota(jnp.int32, (W, 2 * W), 1)
    npair = q_ref.shape[3] // LANES
    tiles = rows // W

    def tile(tt, carry):
        r = tt // tiles
        n = tt % tiles
        r0 = pl.multiple_of(n * W, W)
        dst = pl.ds(r0, W) if dil == 1 else pl.ds(r0 * dil + r, W, stride=dil)
        no_prev = jnp.logical_and(first_chunk, n == 0)
        kill_prev = jnp.logical_and(no_prev, col < W)
        for hp in range(npair):
            ls = slice(hp * LANES, (hp + 1) * LANES)
            q = q_ref[0, r, pl.ds(r0, W), ls]
            kk = kf_ref[r, pl.ds(r0, 2 * W), ls]
            vv = vf_ref[r, pl.ds(r0, 2 * W), ls]
            zero = jnp.zeros_like(q)
            o_h, lse_h = [], []
            for hh in range(2):
                in_head = (lane < HEAD_DIM) if hh == 0 else (lane >= HEAD_DIM)
                qm = jnp.where(in_head, q, zero)
                bias = jnp.where(kill_prev, NEG_BIG, bias_ref[2 * hp + hh])
                s = _dot_nt(qm, kk) + bias
                m = jnp.max(s, axis=-1, keepdims=True)
                p = jnp.exp(s - m)
                l = jnp.sum(p, axis=-1, keepdims=True)
                o_h.append(_dot(p.astype(BF16), vv) / l)
                lse_h.append(m + jnp.log(l))
            o_ref[0, hp, dst, :] = jnp.where(lane < HEAD_DIM, o_h[0], o_h[1])
            lse_ref[0, hp, dst, :] = jnp.where(lane < HEAD_DIM, lse_h[0], lse_h[1])
        return carry

    lax.fori_loop(0, dil * tiles, tile, 0)


def dilated_attention(qkv, bias, *, dil, rows):
    B, _, L, w3 = qkv.shape
    gw = w3 // 3
    W = B_WINDOW_STEPS
    npair = gw // LANES
    tiles = rows // W
    span = rows * dil

    def prev_tile(c):
        return jnp.maximum(c * tiles - 1, 0)

    out = jax.ShapeDtypeStruct((B, npair, L * dil, LANES), F32)
    out_spec = pl.BlockSpec((1, npair, span, LANES), lambda b, c: (b, 0, c, 0))
    return pl.pallas_call(
        functools.partial(_dilated_kernel, dil=dil, rows=rows),
        grid=(B, L // rows),
        in_specs=[
            pl.BlockSpec((1, dil, rows, gw), lambda b, c: (b, 0, c, 0)),
            pl.BlockSpec((1, dil, rows, gw), lambda b, c: (b, 0, c, 1)),
            pl.BlockSpec((1, dil, rows, gw), lambda b, c: (b, 0, c, 2)),
            pl.BlockSpec((1, dil, W, gw), lambda b, c: (b, 0, prev_tile(c), 1)),
            pl.BlockSpec((1, dil, W, gw), lambda b, c: (b, 0, prev_tile(c), 2)),
            pl.BlockSpec(bias.shape, lambda b, c: (0, 0, 0)),
        ],
        out_specs=[out_spec, out_spec],
        out_shape=[out, out],
        scratch_shapes=[pltpu.VMEM((dil, rows + W, gw), BF16),
                        pltpu.VMEM((dil, rows + W, gw), BF16)],
        compiler_params=_params("parallel", "arbitrary"),
        name=f"dilated_attention_d{dil}",
    )(qkv, qkv, qkv, qkv, qkv, bias)


def _dilated_bias(group, dil):
    W = B_WINDOW_STEPS
    n_heads = len(B_GROUP_DILATIONS) * B_HEADS_PER_GROUP
    idx = jnp.arange(1, n_heads + 1, dtype=F32)
    slopes = jnp.exp2(-8.0 * idx / n_heads)[group * B_HEADS_PER_GROUP:
                                             (group + 1) * B_HEADS_PER_GROUP]
    dist = jnp.arange(W)[:, None] + W - jnp.arange(2 * W)[None, :]
    band = (dist >= 0) & (dist <= W)
    bias = -slopes[:, None, None] * (dil * dist).astype(F32)
    return jnp.where(band[None], bias, NEG_BIG)


def _merge_proj_kernel(o0_ref, o1_ref, o2_ref, l0_ref, l1_ref, l2_ref, w_ref, x_ref, out_ref):
    wide = lambda ref: jnp.concatenate([ref[0, hp] for hp in range(ref.shape[1])], axis=1)
    l0, l1, l2 = wide(l0_ref), wide(l1_ref), wide(l2_ref)
    m = jnp.maximum(jnp.maximum(l0, l1), l2)
    e0, e1, e2 = jnp.exp(l0 - m), jnp.exp(l1 - m), jnp.exp(l2 - m)
    o = (e0 * wide(o0_ref) + e1 * wide(o1_ref) + e2 * wide(o2_ref)) / (e0 + e1 + e2)
    out_ref[...] = x_ref[...] + _dot(o.astype(BF16), w_ref[...])


def merge_proj_residual(outs, lses, w, x, *, tm):
    T, D = x.shape
    B, npair, S, _ = outs[0].shape
    gw = w.shape[0]
    tps = S // tm
    row_spec = pl.BlockSpec((1, npair, tm, LANES), lambda i: (i // tps, 0, i % tps, 0))
    return pl.pallas_call(
        _merge_proj_kernel,
        grid=(T // tm,),
        in_specs=[row_spec] * 6 + [pl.BlockSpec((gw, D), lambda i: (0, 0)),
                                   pl.BlockSpec((tm, D), lambda i: (i, 0))],
        out_specs=pl.BlockSpec((tm, D), lambda i: (i, 0)),
        out_shape=jax.ShapeDtypeStruct((T, D), F32),
        compiler_params=_params("parallel"),
        name="merge_proj_residual",
    )(*outs, *lses, w, x)


def _tiles(T, S):
    return dict(
        row_tm=min(512, S),
        dil_span={d: min(S, max(1024, B_WINDOW_STEPS * d)) for d in B_GROUP_DILATIONS},
    )


def kernel(x, a_w_in, a_b_f, a_w_out, b_w_q, b_w_out, kv_norm_g, w_kv, mix_norm_g,
           ffn_norm_g, ffn_w_up, ffn_conv_w, ffn_conv_b, ffn_w_down, final_norm_g):
    B, S, D = x.shape
    T = B * S
    t = _tiles(T, S)
    depth = mix_norm_g.shape[0]
    n_a = a_w_in.shape[0]
    qk_scale = HEAD_DIM ** -0.5
    row = lambda v: v.reshape(1, -1).astype(F32)

    xs = x.reshape(T, D)
    kv_w = None
    for layer in range(depth):
        if layer < n_a:
            hw = A_HEADS * HEAD_DIM
            w_in = a_w_in[layer]
            w_qk = w_in[:, :2 * hw].astype(BF16)
            w_vt = w_in[:, 2 * hw:3 * hw].T.astype(BF16)
            w_f = jnp.pad(w_in[:, 3 * hw:], ((0, 0), (0, LANES - A_HEADS))).astype(BF16)
            b_f = jnp.pad(a_b_f[layer].astype(F32), (0, LANES - A_HEADS)).reshape(1, LANES)
            qk, vt, qaug, kaug = fox_proj(
                xs, row(mix_norm_g[layer]), w_qk, w_vt, w_f, b_f, tm=t["row_tm"], seq_len=S,
                n_heads=A_HEADS, q_cols=hw, scale=qk_scale)
            o = fox_attention(qk.reshape(B, S, 2 * hw), vt, qaug.reshape(B, S, hw),
                              kaug.reshape(B, S, hw))
            xs = proj_residual(o.reshape(T, -1), a_w_out[layer].astype(BF16), xs,
                               tm=t["row_tm"])
        else:
            bl = layer - n_a
            n_g = len(B_GROUP_DILATIONS)
            if kv_w is None:
                gw = w_kv.shape[1] // (2 * n_g)
                kv_w = w_kv.reshape(D, 2, n_g, gw).transpose(0, 2, 1, 3).reshape(D, -1).astype(BF16)
            qkv = b_proj(xs, row(mix_norm_g[layer]), row(kv_norm_g), b_w_q[bl].astype(BF16), kv_w,
                         batch=B, tm=t["row_tm"], dils=B_GROUP_DILATIONS, scale=qk_scale)
            outs, lses = [], []
            for g, dil in enumerate(B_GROUP_DILATIONS):
                o_g, lse_g = dilated_attention(qkv[g], _dilated_bias(g, dil), dil=dil,
                                               rows=t["dil_span"][dil] // dil)
                outs.append(o_g)
                lses.append(lse_g)
            xs = merge_proj_residual(outs, lses, b_w_out[bl].astype(BF16), xs, tm=t["row_tm"])
        last = layer == depth - 1
        xs = conv_ffn(xs, row(ffn_norm_g[layer]), ffn_w_up[layer].astype(BF16),
                      ffn_conv_w[layer].astype(F32), row(ffn_conv_b[layer]),
                      ffn_w_down[layer].astype(BF16), seq_len=S, tm=t["row_tm"],
                      final_g=row(final_norm_g) if last else None)
    return xs.reshape(B, S, D)
```

```python
import functools

import numpy as np
import jax
import jax.numpy as jnp
from jax import lax
from jax.experimental import pallas as pl
from jax.experimental.pallas import tpu as pltpu

F32 = jnp.float32
BF16 = jnp.bfloat16

RMS_EPS = 1e-6
HEAD_DIM = 64
LANES = 128
SUBLANES = 8
A_HEADS = 16
B_GROUP_DILATIONS = (1, 4, 16)
B_HEADS_PER_GROUP = 8
B_WINDOW_STEPS = 128
CONV_WIDTH = 3
NEG_BIG = -1e30
VMEM_LIMIT_BYTES = 56 * 1024 * 1024


def _params(*semantics):
    return pltpu.CompilerParams(dimension_semantics=semantics,
                                vmem_limit_bytes=VMEM_LIMIT_BYTES)


def _rms_norm(x, g):
    y = x * lax.rsqrt(jnp.mean(x * x, axis=-1, keepdims=True) + RMS_EPS)
    return y * g


def _dot(a, b):
    return jnp.dot(a, b, preferred_element_type=F32)


def _dot_nt(a, b):
    return lax.dot_general(a, b, (((1,), (1,)), ((), ())), preferred_element_type=F32)


def _b_proj_kernel(x_ref, gq_ref, gkv_ref, wq_ref, wkv_ref, *rest, dils, scale):
    outs = rest[:len(dils)]
    hq_ref, hkv_ref = rest[len(dils):]
    tm, D = x_ref.shape
    gw = wq_ref.shape[1] // len(dils)
    n_slab = D // LANES
    x = x_ref[...]
    xn = x * lax.rsqrt(jnp.mean(x * x, axis=-1, keepdims=True) + RMS_EPS)
    hq = xn * gq_ref[...]
    hkv = xn * gkv_ref[...]
    for s in range(n_slab):
        hq_ref[s] = hq[:, s * LANES:(s + 1) * LANES]
        hkv_ref[s] = hkv[:, s * LANES:(s + 1) * LANES]

    def residue_major(h_ref, d):
        rows = tm // d
        return jnp.concatenate(
            [jnp.concatenate([h_ref[s, pl.ds(r, rows, stride=d), :] for s in range(n_slab)],
                             axis=1) for r in range(d)], axis=0).astype(BF16)

    for g, d in enumerate(dils):
        if d == 1:
            a_q, a_kv = hq.astype(BF16), hkv.astype(BF16)
        else:
            a_q, a_kv = residue_major(hq_ref, d), residue_major(hkv_ref, d)
        q = (_dot(a_q, wq_ref[:, g * gw:(g + 1) * gw]) * scale).astype(BF16)
        kv = _dot(a_kv, wkv_ref[:, g * 2 * gw:(g + 1) * 2 * gw]).astype(BF16)
        rows = tm // d
        for r in range(d):
            outs[g][0, r, :, :gw] = q[r * rows:(r + 1) * rows]
            outs[g][0, r, :, gw:] = kv[r * rows:(r + 1) * rows]


def b_proj(x, g_q, g_kv, w_q, w_kv, *, batch, tm, dils, scale):
    T, D = x.shape
    S = T // batch
    tps = S // tm
    gw = w_q.shape[1] // len(dils)
    resident = dict(pipeline_mode=pl.Buffered(1))
    return pl.pallas_call(
        functools.partial(_b_proj_kernel, dils=dils, scale=scale),
        grid=(T // tm,),
        in_specs=[pl.BlockSpec((tm, D), lambda i: (i, 0)),
                  pl.BlockSpec((1, D), lambda i: (0, 0)),
                  pl.BlockSpec((1, D), lambda i: (0, 0)),
                  pl.BlockSpec(w_q.shape, lambda i: (0, 0), **resident),
                  pl.BlockSpec(w_kv.shape, lambda i: (0, 0), **resident)],
        out_specs=[pl.BlockSpec((1, d, tm // d, 3 * gw), lambda i: (i // tps, 0, i % tps, 0))
                   for d in dils],
        out_shape=[jax.ShapeDtypeStruct((batch, d, S // d, 3 * gw), BF16) for d in dils],
        scratch_shapes=[pltpu.VMEM((D // LANES, tm, LANES), F32),
                        pltpu.VMEM((D // LANES, tm, LANES), F32)],
        compiler_params=_params("parallel"),
        name="b_proj",
    )(x, g_q, g_kv, w_q, w_kv)


def _split3(x):
    hi = x.astype(BF16)
    rem = x - hi.astype(F32)
    mid = rem.astype(BF16)
    return hi, mid, (rem - mid.astype(F32)).astype(BF16)


def _fox_proj_kernel(x_ref, g_ref, wqk_ref, wvt_ref, wf_ref, bf_ref, eq_ref, ek_ref, cq_ref,
                     ck_ref, qk_ref, vt_ref, qaug_ref, kaug_ref, carry_ref,
                     *, q_cols, scale, log_base, tiles_per_seq):
    tm = x_ref.shape[0]

    @pl.when(pl.program_id(0) == 0)
    def _():
        carry_ref[...] = jnp.zeros_like(carry_ref)

    h = _rms_norm(x_ref[...], g_ref[...]).astype(BF16)
    qk_ref[:, :q_cols] = (_dot(h, wqk_ref[:, :q_cols]) * scale).astype(BF16)
    qk_ref[:, q_cols:] = _dot(h, wqk_ref[:, q_cols:]).astype(BF16)
    vt_ref[...] = _dot_nt(wvt_ref[...], h).astype(BF16)

    z = _dot(h, wf_ref[...]) + bf_ref[...]
    logf = jnp.minimum(z, 0.0) - jnp.log(1.0 + jnp.exp(-jnp.abs(z)))
    r = lax.broadcasted_iota(jnp.int32, (tm, tm), 0)
    c = lax.broadcasted_iota(jnp.int32, (tm, tm), 1)
    tri = jnp.where(r >= c, 1.0, 0.0).astype(BF16)
    p_hi, p_mid, p_lo = _split3(logf)
    first = (pl.program_id(0) % tiles_per_seq) == 0
    prev = jnp.where(first, 0.0, carry_ref[SUBLANES - 1:SUBLANES, :])
    cum = _dot(tri, p_hi) + _dot(tri, p_mid) + _dot(tri, p_lo) + prev
    carry_ref[...] = cum[tm - SUBLANES:]
    c3 = jnp.concatenate(_split3(cum * log_base), axis=1)
    qaug_ref[...] = (_dot(c3, eq_ref[...]) + cq_ref[...]).astype(BF16)
    kaug_ref[...] = (_dot(c3, ek_ref[...]) + ck_ref[...]).astype(BF16)


def _aug_placement(n_heads):
    width = n_heads // 2 * LANES
    eq = np.zeros((3 * LANES, width), np.float32)
    ek = np.zeros((3 * LANES, width), np.float32)
    cq = np.zeros((1, width), np.float32)
    ck = np.zeros((1, width), np.float32)
    for hd in range(n_heads):
        base = hd // 2 * LANES + (HEAD_DIM if hd % 2 == 0 else 0)
        for part in range(3):
            ek[part * LANES + hd, base + part] = 1.0
            eq[part * LANES + hd, base + 3 + part] = 1.0
            cq[0, base + part] = -1.0
            ck[0, base + 3 + part] = 1.0
    return (jnp.asarray(eq, BF16), jnp.asarray(ek, BF16), jnp.asarray(cq), jnp.asarray(ck))


def fox_proj(x, g, w_qk, w_vt, w_f, b_f, *, tm, seq_len, n_heads, q_cols, scale, log_base):
    T, D = x.shape
    n_qk = w_qk.shape[1]
    n_v = w_vt.shape[0]
    aug_w = n_heads // 2 * LANES
    resident = dict(pipeline_mode=pl.Buffered(1))
    const = lambda shape: pl.BlockSpec(shape, lambda i: (0, 0))
    return pl.pallas_call(
        functools.partial(_fox_proj_kernel, q_cols=q_cols, scale=scale, log_base=log_base,
                          tiles_per_seq=seq_len // tm),
        grid=(T // tm,),
        in_specs=[pl.BlockSpec((tm, D), lambda i: (i, 0)),
                  const((1, D)),
                  pl.BlockSpec((D, n_qk), lambda i: (0, 0), **resident),
                  pl.BlockSpec((n_v, D), lambda i: (0, 0), **resident),
                  const((D, LANES)), const((1, LANES)),
                  const((3 * LANES, aug_w)), const((3 * LANES, aug_w)),
                  const((1, aug_w)), const((1, aug_w))],
        out_specs=[pl.BlockSpec((tm, n_qk), lambda i: (i, 0)),
                   pl.BlockSpec((n_v, tm), lambda i: (0, i)),
                   pl.BlockSpec((tm, aug_w), lambda i: (i, 0)),
                   pl.BlockSpec((tm, aug_w), lambda i: (i, 0))],
        out_shape=[jax.ShapeDtypeStruct((T, n_qk), BF16),
                   jax.ShapeDtypeStruct((n_v, T), BF16),
                   jax.ShapeDtypeStruct((T, aug_w), BF16),
                   jax.ShapeDtypeStruct((T, aug_w), BF16)],
        scratch_shapes=[pltpu.VMEM((SUBLANES, LANES), F32)],
        compiler_params=_params("arbitrary"),
        name="fox_proj",
    )(x, g, w_qk, w_vt, w_f, b_f, *_aug_placement(n_heads))


FOX_TQ = 1024
FOX_TK = 256
FOX_DEN_ROWS = 16
LOG2E = 1.4426950408889634


def _row_groups(x):
    return [x[r * SUBLANES:(r + 1) * SUBLANES] for r in range(x.shape[0] // SUBLANES)]


def _tree(op, xs):
    while len(xs) > 1:
        xs = [op(xs[a], xs[a + 1]) for a in range(0, len(xs) - 1, 2)] + (
            [xs[-1]] if len(xs) % 2 else [])
    return xs[0]


def _sublane_allreduce(op, x):
    for shift in (4, 2, 1):
        x = op(x, pltpu.roll(x, shift, axis=0))
    return x


def _tile_rows(x, reps):
    return jnp.concatenate([x] * reps, axis=0)


def _fox_kernel(q_ref, k_ref, vt_ref, qaug_ref, kaug_ref, o_ref,
                qa_ref, ka_ref, m_ref, acc_ref, st_ref, p_ref, alpha_ref):
    S = q_ref.shape[1]
    tq, tk = FOX_TQ, FOX_TK
    lane = lax.broadcasted_iota(jnp.int32, (tq, LANES), 1)

    def augment(c, carry):
        rows = pl.ds(pl.multiple_of(c * tq, tq), tq)
        for hh in range(2):
            in_head = (lane < HEAD_DIM) if hh == 0 else (lane >= HEAD_DIM)
            qa_ref[hh, rows, :] = jnp.where(in_head, q_ref[0, rows, :], qaug_ref[0, rows, :])
            ka_ref[hh, rows, :] = jnp.where(in_head, k_ref[0, rows, :], kaug_ref[0, rows, :])
        return carry

    lax.fori_loop(0, S // tq, augment, 0)
    acc_rows = acc_ref.shape[1]
    ones = jnp.ones((acc_rows - HEAD_DIM, tk), BF16)


    def scores(i, j, lo=0):
        q0 = pl.multiple_of(i * tq + lo, tk)
        k0 = pl.multiple_of(j * tk, tk)
        return [_dot_nt(ka_ref[hh, pl.ds(k0, tk), :], qa_ref[hh, pl.ds(q0, tq - lo), :])
                for hh in range(2)]

    def weighted_values(j, lo=0):
        k0 = pl.multiple_of(j * tk, tk)
        return [_dot(jnp.concatenate(
            [vt_ref[hh * HEAD_DIM:(hh + 1) * HEAD_DIM, pl.ds(k0, tk)], ones], axis=0),
            p_ref[hh, :, lo:]) for hh in range(2)]

    def accumulate(pv, lo=0):
        for hh in range(2):
            acc_ref[hh, :, lo:] = (
                acc_ref[hh, :, lo:] * _tile_rows(alpha_ref[hh, :, lo:], acc_rows // SUBLANES)
                + pv[hh])

    def step(i, j, lo=0, prev_lo=0, diagonal=False, next_lo=None):
        nxt = scores(i, j + 1, next_lo) if next_lo is not None else None
        pv = weighted_values(jnp.maximum(j - 1, 0), prev_lo)
        alphas = []
        for hh in range(2):
            st = st_ref[hh, :, lo:]
            if diagonal:
                key = lax.broadcasted_iota(jnp.int32, st.shape, 0)
                qry = lax.broadcasted_iota(jnp.int32, st.shape, 1)
                st = jnp.where(key <= qry, st, NEG_BIG)
            m_old = m_ref[hh, :, lo:]
            m_new = jnp.maximum(m_old, _sublane_allreduce(
                jnp.maximum, _tree(jnp.maximum, _row_groups(st))))
            alphas.append(jnp.exp2(m_old - m_new))
            m_ref[hh, :, lo:] = m_new
            p_ref[hh, :, lo:] = jnp.exp2(st - _tile_rows(m_new, tk // SUBLANES)).astype(BF16)
        if nxt is not None:
            for hh in range(2):
                st_ref[hh, :, next_lo:] = nxt[hh]
        accumulate(pv, prev_lo)
        for hh in range(2):
            alpha_ref[hh, :, lo:] = alphas[hh]

    per_q = tq // tk

    def kv_body(i, jj, carry):
        for dj in range(per_q):
            step(i, jj * per_q + dj, next_lo=0)
        return carry

    def q_tile(i, carry):
        m_ref[...] = jnp.full(m_ref.shape, NEG_BIG, F32)
        acc_ref[...] = jnp.zeros(acc_ref.shape, F32)
        p_ref[...] = jnp.zeros(p_ref.shape, BF16)
        alpha_ref[...] = jnp.ones(alpha_ref.shape, F32)
        first = scores(i, 0)
        for hh in range(2):
            st_ref[hh] = first[hh]
        lax.fori_loop(0, i, functools.partial(kv_body, i), 0)
        for dj in range(per_q):
            step(i, i * per_q + dj, lo=dj * tk, prev_lo=max(dj - 1, 0) * tk, diagonal=True,
                 next_lo=(dj + 1) * tk if dj + 1 < per_q else None)
        last_lo = (per_q - 1) * tk
        accumulate(weighted_values((i + 1) * per_q - 1, last_lo), last_lo)
        outs = []
        for hh in range(2):
            l = acc_ref[hh, HEAD_DIM:HEAD_DIM + SUBLANES]
            outs.append(acc_ref[hh, :HEAD_DIM] / _tile_rows(l, HEAD_DIM // SUBLANES))
        ot = jnp.concatenate(outs, axis=0)
        o_ref[0, pl.ds(pl.multiple_of(i * tq, tq), tq), :] = ot.T.astype(o_ref.dtype)
        return carry

    lax.fori_loop(0, S // tq, q_tile, 0)


def fox_attention(qk, vt, qaug, kaug):
    B, S, W2 = qk.shape
    npair = W2 // 2 // LANES
    t = FOX_TQ
    return pl.pallas_call(
        _fox_kernel,
        grid=(B, npair),
        in_specs=[
            pl.BlockSpec((1, S, LANES), lambda b, h: (b, 0, h)),
            pl.BlockSpec((1, S, LANES), lambda b, h: (b, 0, npair + h)),
            pl.BlockSpec((LANES, S), lambda b, h: (h, b)),
            pl.BlockSpec((1, S, LANES), lambda b, h: (b, 0, h)),
            pl.BlockSpec((1, S, LANES), lambda b, h: (b, 0, h)),
        ],
        out_specs=pl.BlockSpec((1, S, LANES), lambda b, h: (b, 0, h)),
        out_shape=jax.ShapeDtypeStruct((B, S, npair * LANES), BF16),
        scratch_shapes=[pltpu.VMEM((2, S, LANES), BF16),
                        pltpu.VMEM((2, S, LANES), BF16),
                        pltpu.VMEM((2, SUBLANES, t), F32),
                        pltpu.VMEM((2, HEAD_DIM + FOX_DEN_ROWS, t), F32),
                        pltpu.VMEM((2, FOX_TK, t), F32),
                        pltpu.VMEM((2, FOX_TK, t), BF16),
                        pltpu.VMEM((2, SUBLANES, t), F32)],
        compiler_params=_params("parallel", "parallel"),
        name="fox_attention",
    )(qk, qk, vt, qaug, kaug)


def _proj_residual_kernel(a_ref, w_ref, x_ref, o_ref):
    o_ref[...] = x_ref[...] + _dot(a_ref[...], w_ref[...])


def proj_residual(a, w, x, *, tm):
    T, K = a.shape
    D = w.shape[1]
    return pl.pallas_call(
        _proj_residual_kernel,
        grid=(T // tm,),
        in_specs=[pl.BlockSpec((tm, K), lambda i: (i, 0)),
                  pl.BlockSpec((K, D), lambda i: (0, 0)),
                  pl.BlockSpec((tm, D), lambda i: (i, 0))],
        out_specs=pl.BlockSpec((tm, D), lambda i: (i, 0)),
        out_shape=jax.ShapeDtypeStruct((T, D), F32),
        compiler_params=_params("parallel"),
        name="proj_residual",
    )(a, w, x)


FFN_CHUNK = 256


def _shift_rows(u, prev, shift):
    rolled = pltpu.roll(u, shift, axis=0)
    head = jnp.concatenate([prev, u[:SUBLANES]], axis=0)
    top = pltpu.roll(head, shift, axis=0)[SUBLANES:]
    return jnp.concatenate([top, rolled[SUBLANES:]], axis=0)


def _ffn_kernel(x_ref, g_ref, wup_ref, cw_ref, cb_ref, wd_ref, *rest,
                d_ff, tiles_per_seq, final_norm):
    if final_norm:
        gf_ref, o_ref, h_ref, act_ref, carry_ref = rest
    else:
        o_ref, h_ref, act_ref, carry_ref = rest
    tm = x_ref.shape[0]
    first = (pl.program_id(0) % tiles_per_seq) == 0

    @pl.when(pl.program_id(0) == 0)
    def _():
        carry_ref[...] = jnp.zeros_like(carry_ref)

    x = x_ref[...]
    h_ref[...] = _rms_norm(x, g_ref[...]).astype(BF16)

    def conv_branch(col):
        sl = slice(col, col + FFN_CHUNK)
        u = _dot(h_ref[...], wup_ref[:, sl])
        prev = jnp.where(first, 0.0, carry_ref[:, sl])
        carry_ref[:, sl] = u[tm - SUBLANES:]
        u1 = _shift_rows(u, prev, 1)
        u2 = _shift_rows(u, prev, 2)
        return (cw_ref[0:1, sl] * u2 + cw_ref[1:2, sl] * u1 + cw_ref[2:3, sl] * u
                + cb_ref[:, sl])

    for c in range(d_ff // FFN_CHUNK):
        a = conv_branch(c * FFN_CHUNK)
        gate = conv_branch(d_ff + c * FFN_CHUNK)
        act = gate * (1.0 / (1.0 + jnp.exp(-gate))) * a
        act_ref[:, c * FFN_CHUNK:(c + 1) * FFN_CHUNK] = act.astype(BF16)

    y = x + _dot(act_ref[...], wd_ref[...])
    if final_norm:
        y = _rms_norm(y, gf_ref[...])
    o_ref[...] = y


def conv_ffn(x, g, w_up, conv_w, conv_b, w_down, *, seq_len, tm, final_g=None):
    T, D = x.shape
    d_ff = w_down.shape[0]
    resident = dict(pipeline_mode=pl.Buffered(1))
    in_specs = [
        pl.BlockSpec((tm, D), lambda i: (i, 0)),
        pl.BlockSpec((1, D), lambda i: (0, 0)),
        pl.BlockSpec((D, 2 * d_ff), lambda i: (0, 0), **resident),
        pl.BlockSpec((CONV_WIDTH, 2 * d_ff), lambda i: (0, 0)),
        pl.BlockSpec((1, 2 * d_ff), lambda i: (0, 0)),
        pl.BlockSpec((d_ff, D), lambda i: (0, 0), **resident),
    ]
    args = [x, g, w_up, conv_w, conv_b, w_down]
    if final_g is not None:
        in_specs.append(pl.BlockSpec((1, D), lambda i: (0, 0)))
        args.append(final_g)
    kern = functools.partial(_ffn_kernel, d_ff=d_ff, tiles_per_seq=seq_len // tm,
                             final_norm=final_g is not None)
    return pl.pallas_call(
        kern,
        grid=(T // tm,),
        in_specs=in_specs,
        out_specs=pl.BlockSpec((tm, D), lambda i: (i, 0)),
        out_shape=jax.ShapeDtypeStruct((T, D), F32),
        scratch_shapes=[pltpu.VMEM((tm, D), BF16),
                        pltpu.VMEM((tm, d_ff), BF16),
                        pltpu.VMEM((SUBLANES, 2 * d_ff), F32)],
        compiler_params=_params("arbitrary"),
        name="conv_ffn_final" if final_g is not None else "conv_ffn",
    )(*args)


def _dilated_kernel(q_ref, k_ref, v_ref, kp_ref, vp_ref, bias_ref, o_ref, lse_ref,
                    kf_ref, vf_ref, *, dil, rows):
    W = B_WINDOW_STEPS
    first_chunk = pl.program_id(1) == 0
    kf_ref[:, 0:W] = kp_ref[0]
    kf_ref[:, W:] = k_ref[0]
    vf_ref[:, 0:W] = vp_ref[0]
    vf_ref[:, W:] = v_ref[0]
    lane = lax.broadcasted_iota(jnp.int32, (W, LANES), 1)
    col = lax.broadcasted_iota(jnp.int32, (W, 2 * W), 1)
    npair = q_ref.shape[3] // LANES
    tiles = rows // W

    def tile(tt, carry):
        r = tt // tiles
        n = tt % tiles
        r0 = pl.multiple_of(n * W, W)
        dst = pl.ds(r0, W) if dil == 1 else pl.ds(r0 * dil + r, W, stride=dil)
        no_prev = jnp.logical_and(first_chunk, n == 0)
        kill_prev = jnp.logical_and(no_prev, col < W)
        for hp in range(npair):
            ls = slice(hp * LANES, (hp + 1) * LANES)
            q = q_ref[0, r, pl.ds(r0, W), ls]
            kk = kf_ref[r, pl.ds(r0, 2 * W), ls]
            vv = vf_ref[r, pl.ds(r0, 2 * W), ls]
            zero = jnp.zeros_like(q)
            o_h, lse_h = [], []
            for hh in range(2):
                in_head = (lane < HEAD_DIM) if hh == 0 else (lane >= HEAD_DIM)
                qm = jnp.where(in_head, q, zero)
                bias = jnp.where(kill_prev, NEG_BIG, bias_ref[2 * hp + hh])
                s = _dot_nt(qm, kk) + bias
                m = jnp.max(s, axis=-1, keepdims=True)
                p = jnp.exp(s - m)
                l = jnp.sum(p, axis=-1, keepdims=True)
                o_h.append(_dot(p.astype(BF16), vv) / l)
                lse_h.append(m + jnp.log(l))
            o_ref[0, hp, dst, :] = jnp.where(lane < HEAD_DIM, o_h[0], o_h[1])
            lse_ref[0, hp, dst, :] = jnp.where(lane < HEAD_DIM, lse_h[0], lse_h[1])
        return carry

    lax.fori_loop(0, dil * tiles, tile, 0)


def dilated_attention(qkv, bias, *, dil, rows):
    B, _, L, w3 = qkv.shape
    gw = w3 // 3
    W = B_WINDOW_STEPS
    npair = gw // LANES
    tiles = rows // W
    span = rows * dil

    def prev_tile(c):
        return jnp.maximum(c * tiles - 1, 0)

    out = jax.ShapeDtypeStruct((B, npair, L * dil, LANES), F32)
    out_spec = pl.BlockSpec((1, npair, span, LANES), lambda b, c: (b, 0, c, 0))
    return pl.pallas_call(
        functools.partial(_dilated_kernel, dil=dil, rows=rows),
        grid=(B, L // rows),
        in_specs=[
            pl.BlockSpec((1, dil, rows, gw), lambda b, c: (b, 0, c, 0)),
            pl.BlockSpec((1, dil, rows, gw), lambda b, c: (b, 0, c, 1)),
            pl.BlockSpec((1, dil, rows, gw), lambda b, c: (b, 0, c, 2)),
            pl.BlockSpec((1, dil, W, gw), lambda b, c: (b, 0, prev_tile(c), 1)),
            pl.BlockSpec((1, dil, W, gw), lambda b, c: (b, 0, prev_tile(c), 2)),
            pl.BlockSpec(bias.shape, lambda b, c: (0, 0, 0)),
        ],
        out_specs=[out_spec, out_spec],
        out_shape=[out, out],
        scratch_shapes=[pltpu.VMEM((dil, rows + W, gw), BF16),
                        pltpu.VMEM((dil, rows + W, gw), BF16)],
        compiler_params=_params("parallel", "arbitrary"),
        name=f"dilated_attention_d{dil}",
    )(qkv, qkv, qkv, qkv, qkv, bias)


def _dilated_bias(group, dil):
    W = B_WINDOW_STEPS
    n_heads = len(B_GROUP_DILATIONS) * B_HEADS_PER_GROUP
    idx = jnp.arange(1, n_heads + 1, dtype=F32)
    slopes = jnp.exp2(-8.0 * idx / n_heads)[group * B_HEADS_PER_GROUP:
                                             (group + 1) * B_HEADS_PER_GROUP]
    dist = jnp.arange(W)[:, None] + W - jnp.arange(2 * W)[None, :]
    band = (dist >= 0) & (dist <= W)
    bias = -slopes[:, None, None] * (dil * dist).astype(F32)
    return jnp.where(band[None], bias, NEG_BIG)


def _merge_proj_kernel(o0_ref, o1_ref, o2_ref, l0_ref, l1_ref, l2_ref, w_ref, x_ref, out_ref):
    wide = lambda ref: jnp.concatenate([ref[0, hp] for hp in range(ref.shape[1])], axis=1)
    l0, l1, l2 = wide(l0_ref), wide(l1_ref), wide(l2_ref)
    m = jnp.maximum(jnp.maximum(l0, l1), l2)
    e0, e1, e2 = jnp.exp(l0 - m), jnp.exp(l1 - m), jnp.exp(l2 - m)
    o = (e0 * wide(o0_ref) + e1 * wide(o1_ref) + e2 * wide(o2_ref)) / (e0 + e1 + e2)
    out_ref[...] = x_ref[...] + _dot(o.astype(BF16), w_ref[...])


def merge_proj_residual(outs, lses, w, x, *, tm):
    T, D = x.shape
    B, npair, S, _ = outs[0].shape
    gw = w.shape[0]
    tps = S // tm
    row_spec = pl.BlockSpec((1, npair, tm, LANES), lambda i: (i // tps, 0, i % tps, 0))
    return pl.pallas_call(
        _merge_proj_kernel,
        grid=(T // tm,),
        in_specs=[row_spec] * 6 + [pl.BlockSpec((gw, D), lambda i: (0, 0)),
                                   pl.BlockSpec((tm, D), lambda i: (i, 0))],
        out_specs=pl.BlockSpec((tm, D), lambda i: (i, 0)),
        out_shape=jax.ShapeDtypeStruct((T, D), F32),
        compiler_params=_params("parallel"),
        name="merge_proj_residual",
    )(*outs, *lses, w, x)


def _tiles(T, S):
    return dict(
        row_tm=min(512, S),
        dil_span={d: min(S, max(1024, B_WINDOW_STEPS * d)) for d in B_GROUP_DILATIONS},
    )


def kernel(x, a_w_in, a_b_f, a_w_out, b_w_q, b_w_out, kv_norm_g, w_kv, mix_norm_g,
           ffn_norm_g, ffn_w_up, ffn_conv_w, ffn_conv_b, ffn_w_down, final_norm_g):
    B, S, D = x.shape
    T = B * S
    t = _tiles(T, S)
    depth = mix_norm_g.shape[0]
    n_a = a_w_in.shape[0]
    qk_scale = HEAD_DIM ** -0.5
    row = lambda v: v.reshape(1, -1).astype(F32)

    xs = x.reshape(T, D)
    kv_w = None
    for layer in range(depth):
        if layer < n_a:
            hw = A_HEADS * HEAD_DIM
            w_in = a_w_in[layer]
            w_qk = w_in[:, :2 * hw].astype(BF16)
            w_vt = w_in[:, 2 * hw:3 * hw].T.astype(BF16)
            w_f = jnp.pad(w_in[:, 3 * hw:], ((0, 0), (0, LANES - A_HEADS))).astype(BF16)
            b_f = jnp.pad(a_b_f[layer].astype(F32), (0, LANES - A_HEADS)).reshape(1, LANES)
            qk, vt, qaug, kaug = fox_proj(
                xs, row(mix_norm_g[layer]), w_qk, w_vt, w_f, b_f, tm=t["row_tm"], seq_len=S,
                n_heads=A_HEADS, q_cols=hw, scale=qk_scale * LOG2E, log_base=LOG2E)
            o = fox_attention(qk.reshape(B, S, 2 * hw), vt, qaug.reshape(B, S, hw),
                              kaug.reshape(B, S, hw))
            xs = proj_residual(o.reshape(T, -1), a_w_out[layer].astype(BF16), xs,
                               tm=t["row_tm"])
        else:
            bl = layer - n_a
            n_g = len(B_GROUP_DILATIONS)
            if kv_w is None:
                gw = w_kv.shape[1] // (2 * n_g)
                kv_w = w_kv.reshape(D, 2, n_g, gw).transpose(0, 2, 1, 3).reshape(D, -1).astype(BF16)
            qkv = b_proj(xs, row(mix_norm_g[layer]), row(kv_norm_g), b_w_q[bl].astype(BF16), kv_w,
                         batch=B, tm=t["row_tm"], dils=B_GROUP_DILATIONS, scale=qk_scale)
            outs, lses = [], []
            for g, dil in enumerate(B_GROUP_DILATIONS):
                o_g, lse_g = dilated_attention(qkv[g], _dilated_bias(g, dil), dil=dil,
                                               rows=t["dil_span"][dil] // dil)
                outs.append(o_g)
                lses.append(lse_g)
            xs = merge_proj_residual(outs, lses, b_w_out[bl].astype(BF16), xs, tm=t["row_tm"])
        last = layer == depth - 1
        xs = conv_ffn(xs, row(ffn_norm_g[layer]), ffn_w_up[layer].astype(BF16),
                      ffn_conv_w[layer].astype(F32), row(ffn_conv_b[layer]),
                      ffn_w_down[layer].astype(BF16), seq_len=S, tm=t["row_tm"],
                      final_g=row(final_norm_g) if last else None)
    return xs.reshape(B, S, D)
```

```python
import functools

import numpy as np
import jax
import jax.numpy as jnp
from jax import lax
from jax.experimental import pallas as pl
from jax.experimental.pallas import tpu as pltpu

F32 = jnp.float32
BF16 = jnp.bfloat16

RMS_EPS = 1e-6
HEAD_DIM = 64
LANES = 128
SUBLANES = 8
A_HEADS = 16
B_GROUP_DILATIONS = (1, 4, 16)
B_HEADS_PER_GROUP = 8
B_WINDOW_STEPS = 128
CONV_WIDTH = 3
NEG_BIG = -1e30
VMEM_LIMIT_BYTES = 56 * 1024 * 1024


def _params(*semantics):
    return pltpu.CompilerParams(dimension_semantics=semantics,
                                vmem_limit_bytes=VMEM_LIMIT_BYTES)


def _rms_norm(x, g):
    y = x * lax.rsqrt(jnp.mean(x * x, axis=-1, keepdims=True) + RMS_EPS)
    return y * g


def _dot(a, b):
    return jnp.dot(a, b, preferred_element_type=F32)


def _dot_nt(a, b):
    return lax.dot_general(a, b, (((1,), (1,)), ((), ())), preferred_element_type=F32)


def _b_proj_kernel(x_ref, gq_ref, gkv_ref, wq_ref, wkv_ref, *rest, dils, scale):
    outs = rest[:len(dils)]
    hq_ref, hkv_ref = rest[len(dils):]
    tm, D = x_ref.shape
    gw = wq_ref.shape[1] // len(dils)
    n_slab = D // LANES
    x = x_ref[...]
    xn = x * lax.rsqrt(jnp.mean(x * x, axis=-1, keepdims=True) + RMS_EPS)
    hq = xn * gq_ref[...]
    hkv = xn * gkv_ref[...]
    for s in range(n_slab):
        hq_ref[s] = hq[:, s * LANES:(s + 1) * LANES]
        hkv_ref[s] = hkv[:, s * LANES:(s + 1) * LANES]

    def residue_major(h_ref, d):
        rows = tm // d
        return jnp.concatenate(
            [jnp.concatenate([h_ref[s, pl.ds(r, rows, stride=d), :] for s in range(n_slab)],
                             axis=1) for r in range(d)], axis=0).astype(BF16)

    for g, d in enumerate(dils):
        if d == 1:
            a_q, a_kv = hq.astype(BF16), hkv.astype(BF16)
        else:
            a_q, a_kv = residue_major(hq_ref, d), residue_major(hkv_ref, d)
        q = (_dot(a_q, wq_ref[:, g * gw:(g + 1) * gw]) * scale).astype(BF16)
        kv = _dot(a_kv, wkv_ref[:, g * 2 * gw:(g + 1) * 2 * gw]).astype(BF16)
        rows = tm // d
        for r in range(d):
            outs[g][0, r, :, :gw] = q[r * rows:(r + 1) * rows]
            outs[g][0, r, :, gw:] = kv[r * rows:(r + 1) * rows]


def b_proj(x, g_q, g_kv, w_q, w_kv, *, batch, tm, dils, scale):
    T, D = x.shape
    S = T // batch
    tps = S // tm
    gw = w_q.shape[1] // len(dils)
    resident = dict(pipeline_mode=pl.Buffered(1))
    return pl.pallas_call(
        functools.partial(_b_proj_kernel, dils=dils, scale=scale),
        grid=(T // tm,),
        in_specs=[pl.BlockSpec((tm, D), lambda i: (i, 0)),
                  pl.BlockSpec((1, D), lambda i: (0, 0)),
                  pl.BlockSpec((1, D), lambda i: (0, 0)),
                  pl.BlockSpec(w_q.shape, lambda i: (0, 0), **resident),
                  pl.BlockSpec(w_kv.shape, lambda i: (0, 0), **resident)],
        out_specs=[pl.BlockSpec((1, d, tm // d, 3 * gw), lambda i: (i // tps, 0, i % tps, 0))
                   for d in dils],
        out_shape=[jax.ShapeDtypeStruct((batch, d, S // d, 3 * gw), BF16) for d in dils],
        scratch_shapes=[pltpu.VMEM((D // LANES, tm, LANES), F32),
                        pltpu.VMEM((D // LANES, tm, LANES), F32)],
        compiler_params=_params("parallel"),
        name="b_proj",
    )(x, g_q, g_kv, w_q, w_kv)


def _split3(x):
    hi = x.astype(BF16)
    rem = x - hi.astype(F32)
    mid = rem.astype(BF16)
    return hi, mid, (rem - mid.astype(F32)).astype(BF16)


def _fox_proj_kernel(x_ref, g_ref, wqk_ref, wvt_ref, wf_ref, bf_ref, eq_ref, ek_ref, cq_ref,
                     ck_ref, qk_ref, vt_ref, qaug_ref, kaug_ref, carry_ref,
                     *, q_cols, scale, log_base, n_heads, tiles_per_seq):
    tm = x_ref.shape[0]

    @pl.when(pl.program_id(0) == 0)
    def _():
        carry_ref[...] = jnp.zeros_like(carry_ref)

    h = _rms_norm(x_ref[...], g_ref[...]).astype(BF16)

    z = _dot(h, wf_ref[...]) + bf_ref[...]
    logf = jnp.minimum(z, 0.0) - jnp.log(1.0 + jnp.exp(-jnp.abs(z)))
    r = lax.broadcasted_iota(jnp.int32, (tm, tm), 0)
    c = lax.broadcasted_iota(jnp.int32, (tm, tm), 1)
    tri = jnp.where(r >= c, 1.0, 0.0).astype(BF16)
    p_hi, p_mid, p_lo = _split3(logf)
    first = (pl.program_id(0) % tiles_per_seq) == 0
    prev = jnp.where(first, 0.0, carry_ref[SUBLANES - 1:SUBLANES, :])
    cum = _dot(tri, p_hi) + _dot(tri, p_mid) + _dot(tri, p_lo) + prev
    carry_ref[...] = cum[tm - SUBLANES:]
    hi, mid, lo = (t.astype(F32) for t in _split3(cum * log_base))
    lane = lax.broadcasted_iota(jnp.int32, cum.shape, 1)
    packed = jnp.where(lane < n_heads, hi,
                       jnp.where(lane < 2 * n_heads, pltpu.roll(mid, n_heads, axis=1),
                                 pltpu.roll(lo, 2 * n_heads, axis=1))).astype(BF16)
    qaug_ref[...] = (_dot(packed, eq_ref[...]) + cq_ref[...]).astype(BF16)
    kaug_ref[...] = (_dot(packed, ek_ref[...]) + ck_ref[...]).astype(BF16)

    qk_ref[:, :q_cols] = (_dot(h, wqk_ref[:, :q_cols]) * scale).astype(BF16)
    qk_ref[:, q_cols:] = _dot(h, wqk_ref[:, q_cols:]).astype(BF16)
    vt_ref[...] = _dot_nt(wvt_ref[...], h).astype(BF16)


def _aug_placement(n_heads):
    assert 3 * n_heads <= LANES
    width = n_heads // 2 * LANES
    eq = np.zeros((LANES, width), np.float32)
    ek = np.zeros((LANES, width), np.float32)
    cq = np.zeros((1, width), np.float32)
    ck = np.zeros((1, width), np.float32)
    for hd in range(n_heads):
        base = hd // 2 * LANES + (HEAD_DIM if hd % 2 == 0 else 0)
        for part in range(3):
            ek[part * n_heads + hd, base + part] = 1.0
            eq[part * n_heads + hd, base + 3 + part] = 1.0
            cq[0, base + part] = -1.0
            ck[0, base + 3 + part] = 1.0
    return (jnp.asarray(eq, BF16), jnp.asarray(ek, BF16), jnp.asarray(cq), jnp.asarray(ck))


def fox_proj(x, g, w_qk, w_vt, w_f, b_f, *, tm, seq_len, n_heads, q_cols, scale, log_base):
    T, D = x.shape
    n_qk = w_qk.shape[1]
    n_v = w_vt.shape[0]
    aug_w = n_heads // 2 * LANES
    resident = dict(pipeline_mode=pl.Buffered(1))
    const = lambda shape: pl.BlockSpec(shape, lambda i: (0, 0))
    return pl.pallas_call(
        functools.partial(_fox_proj_kernel, q_cols=q_cols, scale=scale, log_base=log_base,
                          n_heads=n_heads,
                          tiles_per_seq=seq_len // tm),
        grid=(T // tm,),
        in_specs=[pl.BlockSpec((tm, D), lambda i: (i, 0)),
                  const((1, D)),
                  pl.BlockSpec((D, n_qk), lambda i: (0, 0), **resident),
                  pl.BlockSpec((n_v, D), lambda i: (0, 0), **resident),
                  const((D, LANES)), const((1, LANES)),
                  const((LANES, aug_w)), const((LANES, aug_w)),
                  const((1, aug_w)), const((1, aug_w))],
        out_specs=[pl.BlockSpec((tm, n_qk), lambda i: (i, 0)),
                   pl.BlockSpec((n_v, tm), lambda i: (0, i)),
                   pl.BlockSpec((tm, aug_w), lambda i: (i, 0)),
                   pl.BlockSpec((tm, aug_w), lambda i: (i, 0))],
        out_shape=[jax.ShapeDtypeStruct((T, n_qk), BF16),
                   jax.ShapeDtypeStruct((n_v, T), BF16),
                   jax.ShapeDtypeStruct((T, aug_w), BF16),
                   jax.ShapeDtypeStruct((T, aug_w), BF16)],
        scratch_shapes=[pltpu.VMEM((SUBLANES, LANES), F32)],
        compiler_params=_params("arbitrary"),
        name="fox_proj",
    )(x, g, w_qk, w_vt, w_f, b_f, *_aug_placement(n_heads))


FOX_TQ = 1024
FOX_TK = 256
FOX_DEN_ROWS = 16
LOG2E = 1.4426950408889634


def _row_groups(x):
    return [x[r * SUBLANES:(r + 1) * SUBLANES] for r in range(x.shape[0] // SUBLANES)]


def _tree(op, xs):
    while len(xs) > 1:
        xs = [op(xs[a], xs[a + 1]) for a in range(0, len(xs) - 1, 2)] + (
            [xs[-1]] if len(xs) % 2 else [])
    return xs[0]


def _sublane_allreduce(op, x):
    for shift in (4, 2, 1):
        x = op(x, pltpu.roll(x, shift, axis=0))
    return x


def _tile_rows(x, reps):
    return jnp.concatenate([x] * reps, axis=0)


def _fox_kernel(q_ref, k_ref, vt_ref, qaug_ref, kaug_ref, o_ref,
                qa_ref, ka_ref, m_ref, acc_ref, st_ref, p_ref, alpha_ref):
    S = q_ref.shape[1]
    tq, tk = FOX_TQ, FOX_TK
    lane = lax.broadcasted_iota(jnp.int32, (tq, LANES), 1)

    def augment(c, carry):
        rows = pl.ds(pl.multiple_of(c * tq, tq), tq)
        for hh in range(2):
            in_head = (lane < HEAD_DIM) if hh == 0 else (lane >= HEAD_DIM)
            qa_ref[hh, rows, :] = jnp.where(in_head, q_ref[0, rows, :], qaug_ref[0, rows, :])
            ka_ref[hh, rows, :] = jnp.where(in_head, k_ref[0, rows, :], kaug_ref[0, rows, :])
        return carry

    lax.fori_loop(0, S // tq, augment, 0)
    acc_rows = acc_ref.shape[1]
    ones = jnp.ones((acc_rows - HEAD_DIM, tk), BF16)


    def scores(i, j, lo=0):
        q0 = pl.multiple_of(i * tq + lo, tk)
        k0 = pl.multiple_of(j * tk, tk)
        return [_dot_nt(ka_ref[hh, pl.ds(k0, tk), :], qa_ref[hh, pl.ds(q0, tq - lo), :])
                for hh in range(2)]

    def weighted_values(j, lo=0):
        k0 = pl.multiple_of(j * tk, tk)
        return [_dot(jnp.concatenate(
            [vt_ref[hh * HEAD_DIM:(hh + 1) * HEAD_DIM, pl.ds(k0, tk)], ones], axis=0),
            p_ref[hh, :, lo:]) for hh in range(2)]

    def accumulate(pv, lo=0):
        for hh in range(2):
            acc_ref[hh, :, lo:] = (
                acc_ref[hh, :, lo:] * _tile_rows(alpha_ref[hh, :, lo:], acc_rows // SUBLANES)
                + pv[hh])

    def step(i, j, lo=0, prev_lo=0, diagonal=False, next_lo=None):
        nxt = scores(i, j + 1, next_lo) if next_lo is not None else None
        pv = weighted_values(jnp.maximum(j - 1, 0), prev_lo)
        alphas = []
        for hh in range(2):
            st = st_ref[hh, :, lo:]
            if diagonal:
                key = lax.broadcasted_iota(jnp.int32, st.shape, 0)
                qry = lax.broadcasted_iota(jnp.int32, st.shape, 1)
                st = jnp.where(key <= qry, st, NEG_BIG)
            m_old = m_ref[hh, :, lo:]
            m_new = jnp.maximum(m_old, _sublane_allreduce(
                jnp.maximum, _tree(jnp.maximum, _row_groups(st))))
            alphas.append(jnp.exp2(m_old - m_new))
            m_ref[hh, :, lo:] = m_new
            p_ref[hh, :, lo:] = jnp.exp2(st - _tile_rows(m_new, tk // SUBLANES)).astype(BF16)
        if nxt is not None:
            for hh in range(2):
                st_ref[hh, :, next_lo:] = nxt[hh]
        accumulate(pv, prev_lo)
        for hh in range(2):
            alpha_ref[hh, :, lo:] = alphas[hh]

    per_q = tq // tk

    def kv_body(i, jj, carry):
        for dj in range(per_q):
            step(i, jj * per_q + dj, next_lo=0)
        return carry

    def q_tile(i, carry):
        m_ref[...] = jnp.full(m_ref.shape, NEG_BIG, F32)
        acc_ref[...] = jnp.zeros(acc_ref.shape, F32)
        p_ref[...] = jnp.zeros(p_ref.shape, BF16)
        alpha_ref[...] = jnp.ones(alpha_ref.shape, F32)
        first = scores(i, 0)
        for hh in range(2):
            st_ref[hh] = first[hh]
        lax.fori_loop(0, i, functools.partial(kv_body, i), 0)
        for dj in range(per_q):
            step(i, i * per_q + dj, lo=dj * tk, prev_lo=max(dj - 1, 0) * tk, diagonal=True,
                 next_lo=(dj + 1) * tk if dj + 1 < per_q else None)
        last_lo = (per_q - 1) * tk
        accumulate(weighted_values((i + 1) * per_q - 1, last_lo), last_lo)
        outs = []
        for hh in range(2):
            l = acc_ref[hh, HEAD_DIM:HEAD_DIM + SUBLANES]
            outs.append(acc_ref[hh, :HEAD_DIM] / _tile_rows(l, HEAD_DIM // SUBLANES))
        ot = jnp.concatenate(outs, axis=0)
        o_ref[0, pl.ds(pl.multiple_of(i * tq, tq), tq), :] = ot.T.astype(o_ref.dtype)
        return carry

    lax.fori_loop(0, S // tq, q_tile, 0)


def fox_attention(qk, vt, qaug, kaug):
    B, S, W2 = qk.shape
    npair = W2 // 2 // LANES
    t = FOX_TQ
    return pl.pallas_call(
        _fox_kernel,
        grid=(B, npair),
        in_specs=[
            pl.BlockSpec((1, S, LANES), lambda b, h: (b, 0, h)),
            pl.BlockSpec((1, S, LANES), lambda b, h: (b, 0, npair + h)),
            pl.BlockSpec((LANES, S), lambda b, h: (h, b)),
            pl.BlockSpec((1, S, LANES), lambda b, h: (b, 0, h)),
            pl.BlockSpec((1, S, LANES), lambda b, h: (b, 0, h)),
        ],
        out_specs=pl.BlockSpec((1, S, LANES), lambda b, h: (b, 0, h)),
        out_shape=jax.ShapeDtypeStruct((B, S, npair * LANES), BF16),
        scratch_shapes=[pltpu.VMEM((2, S, LANES), BF16),
                        pltpu.VMEM((2, S, LANES), BF16),
                        pltpu.VMEM((2, SUBLANES, t), F32),
                        pltpu.VMEM((2, HEAD_DIM + FOX_DEN_ROWS, t), F32),
                        pltpu.VMEM((2, FOX_TK, t), F32),
                        pltpu.VMEM((2, FOX_TK, t), BF16),
                        pltpu.VMEM((2, SUBLANES, t), F32)],
        compiler_params=_params("parallel", "parallel"),
        name="fox_attention",
    )(qk, qk, vt, qaug, kaug)


FFN_CHUNK = 256


def _shift_rows(u, prev, shift):
    rolled = pltpu.roll(u, shift, axis=0)
    head = jnp.concatenate([prev, u[:SUBLANES]], axis=0)
    top = pltpu.roll(head, shift, axis=0)[SUBLANES:]
    return jnp.concatenate([top, rolled[SUBLANES:]], axis=0)


def _merged_groups(refs):
    wide = lambda ref: jnp.concatenate([ref[0, hp] for hp in range(ref.shape[1])], axis=1)
    n = len(refs) // 2
    outs, lses = [wide(r) for r in refs[:n]], [wide(r) for r in refs[n:]]
    m = functools.reduce(jnp.maximum, lses)
    es = [jnp.exp2(l - m) for l in lses]
    return sum(e * o for e, o in zip(es, outs)) / sum(es)


def _ffn_kernel(x_ref, *rest, n_mix, d_ff, tiles_per_seq, final_norm):
    mix_refs, rest = rest[:n_mix], rest[n_mix:]
    wo_ref, g_ref, wup_ref, cw_ref, cb_ref, wd_ref = rest[:6]
    rest = rest[6:]
    if final_norm:
        gf_ref, o_ref, h_ref, act_ref, carry_ref = rest
    else:
        o_ref, h_ref, act_ref, carry_ref = rest
    tm = x_ref.shape[0]
    first = (pl.program_id(0) % tiles_per_seq) == 0

    @pl.when(pl.program_id(0) == 0)
    def _():
        carry_ref[...] = jnp.zeros_like(carry_ref)

    mixed = mix_refs[0][...] if n_mix == 1 else _merged_groups(mix_refs).astype(BF16)
    x = x_ref[...] + _dot(mixed, wo_ref[...])
    h_ref[...] = _rms_norm(x, g_ref[...]).astype(BF16)

    def conv_branch(col):
        sl = slice(col, col + FFN_CHUNK)
        u = _dot(h_ref[...], wup_ref[:, sl])
        prev = jnp.where(first, 0.0, carry_ref[:, sl])
        carry_ref[:, sl] = u[tm - SUBLANES:]
        u1 = _shift_rows(u, prev, 1)
        u2 = _shift_rows(u, prev, 2)
        return (cw_ref[0:1, sl] * u2 + cw_ref[1:2, sl] * u1 + cw_ref[2:3, sl] * u
                + cb_ref[:, sl])

    for c in range(d_ff // FFN_CHUNK):
        a = conv_branch(c * FFN_CHUNK)
        gate = conv_branch(d_ff + c * FFN_CHUNK)
        act = gate * (1.0 / (1.0 + jnp.exp(-gate))) * a
        act_ref[:, c * FFN_CHUNK:(c + 1) * FFN_CHUNK] = act.astype(BF16)

    y = x + _dot(act_ref[...], wd_ref[...])
    if final_norm:
        y = _rms_norm(y, gf_ref[...])
    o_ref[...] = y


def mixer_out_conv_ffn(x, mixed, w_out, g, w_up, conv_w, conv_b, w_down, *, seq_len, tm,
                       final_g=None):
    T, D = x.shape
    d_ff = w_down.shape[0]
    tps = seq_len // tm
    resident = dict(pipeline_mode=pl.Buffered(1))
    if len(mixed) == 1:
        mix_specs = [pl.BlockSpec((tm, mixed[0].shape[1]), lambda i: (i, 0))]
    else:
        npair = mixed[0].shape[1]
        mix_specs = [pl.BlockSpec((1, npair, tm, LANES),
                                  lambda i: (i // tps, 0, i % tps, 0))] * len(mixed)
    in_specs = [pl.BlockSpec((tm, D), lambda i: (i, 0))] + mix_specs + [
        pl.BlockSpec(w_out.shape, lambda i: (0, 0), **resident),
        pl.BlockSpec((1, D), lambda i: (0, 0)),
        pl.BlockSpec((D, 2 * d_ff), lambda i: (0, 0), **resident),
        pl.BlockSpec((CONV_WIDTH, 2 * d_ff), lambda i: (0, 0)),
        pl.BlockSpec((1, 2 * d_ff), lambda i: (0, 0)),
        pl.BlockSpec((d_ff, D), lambda i: (0, 0), **resident),
    ]
    args = [x, *mixed, w_out, g, w_up, conv_w, conv_b, w_down]
    if final_g is not None:
        in_specs.append(pl.BlockSpec((1, D), lambda i: (0, 0)))
        args.append(final_g)
    kern = functools.partial(_ffn_kernel, n_mix=len(mixed), d_ff=d_ff, tiles_per_seq=tps,
                             final_norm=final_g is not None)
    return pl.pallas_call(
        kern,
        grid=(T // tm,),
        in_specs=in_specs,
        out_specs=pl.BlockSpec((tm, D), lambda i: (i, 0)),
        out_shape=jax.ShapeDtypeStruct((T, D), F32),
        scratch_shapes=[pltpu.VMEM((tm, D), BF16),
                        pltpu.VMEM((tm, d_ff), BF16),
                        pltpu.VMEM((SUBLANES, 2 * d_ff), F32)],
        compiler_params=_params("arbitrary"),
        name="conv_ffn_final" if final_g is not None else "conv_ffn",
    )(*args)


DILATED_TILES_PER_TRIP = 4


def _dilated_kernel(q_ref, k_ref, v_ref, kp_ref, vp_ref, bias_ref, o_ref, lse_ref,
                    *, dil, rows):
    W = B_WINDOW_STEPS
    first_chunk = pl.program_id(1) == 0
    lane = lax.broadcasted_iota(jnp.int32, (W, LANES), 1)
    col = lax.broadcasted_iota(jnp.int32, (W, 2 * W), 1)
    npair = q_ref.shape[3] // LANES
    tiles = rows // W

    def tile(tt, carry):
        r = tt // tiles
        n = tt % tiles
        r0 = pl.multiple_of(n * W, W)
        prev = pl.ds(pl.multiple_of(jnp.maximum(n - 1, 0) * W, W), W)
        dst = pl.ds(r0, W) if dil == 1 else pl.ds(r0 * dil + r, W, stride=dil)
        no_prev = jnp.logical_and(first_chunk, n == 0)
        kill_prev = jnp.logical_and(no_prev, col < W)
        for hp in range(npair):
            ls = slice(hp * LANES, (hp + 1) * LANES)
            q = q_ref[0, r, pl.ds(r0, W), ls]
            kk = jnp.concatenate([jnp.where(n == 0, kp_ref[0, r, :, ls], k_ref[0, r, prev, ls]),
                                  k_ref[0, r, pl.ds(r0, W), ls]], axis=0)
            vv = jnp.concatenate([jnp.where(n == 0, vp_ref[0, r, :, ls], v_ref[0, r, prev, ls]),
                                  v_ref[0, r, pl.ds(r0, W), ls]], axis=0)
            zero = jnp.zeros_like(q)
            o_h, lse_h = [], []
            for hh in range(2):
                in_head = (lane < HEAD_DIM) if hh == 0 else (lane >= HEAD_DIM)
                qm = jnp.where(in_head, q, zero)
                bias = jnp.where(kill_prev, NEG_BIG, bias_ref[2 * hp + hh])
                s = _dot_nt(qm, kk) + bias
                m = jnp.max(s, axis=-1, keepdims=True)
                p = jnp.exp2(s - m)
                l = jnp.sum(p, axis=-1, keepdims=True)
                o_h.append(_dot(p.astype(BF16), vv) / l)
                lse_h.append(m + jnp.log2(l))
            o_ref[0, hp, dst, :] = jnp.where(lane < HEAD_DIM, o_h[0], o_h[1])
            lse_ref[0, hp, dst, :] = jnp.where(lane < HEAD_DIM, lse_h[0], lse_h[1])
        return carry

    def tile_group(tg, carry):
        for u in range(DILATED_TILES_PER_TRIP):
            tile(tg * DILATED_TILES_PER_TRIP + u, carry)
        return carry

    lax.fori_loop(0, dil * tiles // DILATED_TILES_PER_TRIP, tile_group, 0)


def dilated_attention(qkv, bias, *, dil, rows):
    B, _, L, w3 = qkv.shape
    gw = w3 // 3
    W = B_WINDOW_STEPS
    npair = gw // LANES
    tiles = rows // W
    span = rows * dil

    def prev_tile(c):
        return jnp.maximum(c * tiles - 1, 0)

    out = jax.ShapeDtypeStruct((B, npair, L * dil, LANES), F32)
    out_spec = pl.BlockSpec((1, npair, span, LANES), lambda b, c: (b, 0, c, 0))
    return pl.pallas_call(
        functools.partial(_dilated_kernel, dil=dil, rows=rows),
        grid=(B, L // rows),
        in_specs=[
            pl.BlockSpec((1, dil, rows, gw), lambda b, c: (b, 0, c, 0)),
            pl.BlockSpec((1, dil, rows, gw), lambda b, c: (b, 0, c, 1)),
            pl.BlockSpec((1, dil, rows, gw), lambda b, c: (b, 0, c, 2)),
            pl.BlockSpec((1, dil, W, gw), lambda b, c: (b, 0, prev_tile(c), 1)),
            pl.BlockSpec((1, dil, W, gw), lambda b, c: (b, 0, prev_tile(c), 2)),
            pl.BlockSpec(bias.shape, lambda b, c: (0, 0, 0)),
        ],
        out_specs=[out_spec, out_spec],
        out_shape=[out, out],
        compiler_params=_params("parallel", "arbitrary"),
        name=f"dilated_attention_d{dil}",
    )(qkv, qkv, qkv, qkv, qkv, bias)


def _dilated_bias(group, dil):
    W = B_WINDOW_STEPS
    n_heads = len(B_GROUP_DILATIONS) * B_HEADS_PER_GROUP
    idx = jnp.arange(1, n_heads + 1, dtype=F32)
    slopes = jnp.exp2(-8.0 * idx / n_heads)[group * B_HEADS_PER_GROUP:
                                             (group + 1) * B_HEADS_PER_GROUP]
    dist = jnp.arange(W)[:, None] + W - jnp.arange(2 * W)[None, :]
    band = (dist >= 0) & (dist <= W)
    bias = -slopes[:, None, None] * (dil * dist).astype(F32)
    return jnp.where(band[None], bias * LOG2E, NEG_BIG)


def _tiles(T, S):
    return dict(
        row_tm=min(512, S),
        dil_span={d: min(S, max(1024, B_WINDOW_STEPS * d)) for d in B_GROUP_DILATIONS},
    )


def kernel(x, a_w_in, a_b_f, a_w_out, b_w_q, b_w_out, kv_norm_g, w_kv, mix_norm_g,
           ffn_norm_g, ffn_w_up, ffn_conv_w, ffn_conv_b, ffn_w_down, final_norm_g):
    B, S, D = x.shape
    T = B * S
    t = _tiles(T, S)
    depth = mix_norm_g.shape[0]
    n_a = a_w_in.shape[0]
    qk_scale = HEAD_DIM ** -0.5
    row = lambda v: v.reshape(1, -1).astype(F32)

    xs = x.reshape(T, D)
    kv_w = None
    for layer in range(depth):
        if layer < n_a:
            hw = A_HEADS * HEAD_DIM
            w_in = a_w_in[layer]
            w_qk = w_in[:, :2 * hw].astype(BF16)
            w_vt = w_in[:, 2 * hw:3 * hw].T.astype(BF16)
            w_f = jnp.pad(w_in[:, 3 * hw:], ((0, 0), (0, LANES - A_HEADS))).astype(BF16)
            b_f = jnp.pad(a_b_f[layer].astype(F32), (0, LANES - A_HEADS)).reshape(1, LANES)
            qk, vt, qaug, kaug = fox_proj(
                xs, row(mix_norm_g[layer]), w_qk, w_vt, w_f, b_f, tm=t["row_tm"], seq_len=S,
                n_heads=A_HEADS, q_cols=hw, scale=qk_scale * LOG2E, log_base=LOG2E)
            o = fox_attention(qk.reshape(B, S, 2 * hw), vt, qaug.reshape(B, S, hw),
                              kaug.reshape(B, S, hw))
            mixed, w_out = [o.reshape(T, -1)], a_w_out[layer].astype(BF16)
        else:
            bl = layer - n_a
            n_g = len(B_GROUP_DILATIONS)
            if kv_w is None:
                gw = w_kv.shape[1] // (2 * n_g)
                kv_w = w_kv.reshape(D, 2, n_g, gw).transpose(0, 2, 1, 3).reshape(D, -1).astype(BF16)
            qkv = b_proj(xs, row(mix_norm_g[layer]), row(kv_norm_g), b_w_q[bl].astype(BF16), kv_w,
                         batch=B, tm=t["row_tm"], dils=B_GROUP_DILATIONS, scale=qk_scale * LOG2E)
            outs, lses = [], []
            for g, dil in enumerate(B_GROUP_DILATIONS):
                o_g, lse_g = dilated_attention(qkv[g], _dilated_bias(g, dil), dil=dil,
                                               rows=t["dil_span"][dil] // dil)
                outs.append(o_g)
                lses.append(lse_g)
            mixed, w_out = outs + lses, b_w_out[bl].astype(BF16)
        last = layer == depth - 1
        xs = mixer_out_conv_ffn(
            xs, mixed, w_out, row(ffn_norm_g[layer]), ffn_w_up[layer].astype(BF16),
            ffn_conv_w[layer].astype(F32), row(ffn_conv_b[layer]),
            ffn_w_down[layer].astype(BF16), seq_len=S, tm=t["row_tm"],
            final_g=row(final_norm_g) if last else None)
    return xs.reshape(B, S, D)
```

```python
import functools

import numpy as np
import jax
import jax.numpy as jnp
from jax import lax
from jax.experimental import pallas as pl
from jax.experimental.pallas import tpu as pltpu

F32 = jnp.float32
BF16 = jnp.bfloat16

RMS_EPS = 1e-6
HEAD_DIM = 64
LANES = 128
SUBLANES = 8
A_HEADS = 16
B_GROUP_DILATIONS = (1, 4, 16)
B_HEADS_PER_GROUP = 8
B_WINDOW_STEPS = 128
CONV_WIDTH = 3
NEG_BIG = -1e30
VMEM_LIMIT_BYTES = 56 * 1024 * 1024


def _params(*semantics):
    return pltpu.CompilerParams(dimension_semantics=semantics,
                                vmem_limit_bytes=VMEM_LIMIT_BYTES)


def _rms_norm(x, g):
    y = x * lax.rsqrt(jnp.mean(x * x, axis=-1, keepdims=True) + RMS_EPS)
    return y * g


def _dot(a, b):
    return jnp.dot(a, b, preferred_element_type=F32)


def _dot_nt(a, b):
    return lax.dot_general(a, b, (((1,), (1,)), ((), ())), preferred_element_type=F32)


def _b_proj_kernel(x_ref, gq_ref, gkv_ref, wq_ref, wkv_ref, *rest, dils, scale):
    outs = rest[:len(dils)]
    hq_ref, hkv_ref = rest[len(dils):]
    tm, D = x_ref.shape
    gw = wq_ref.shape[1] // len(dils)
    n_slab = D // LANES
    x = x_ref[...]
    xn = x * lax.rsqrt(jnp.mean(x * x, axis=-1, keepdims=True) + RMS_EPS)
    hq = xn * gq_ref[...]
    hkv = xn * gkv_ref[...]
    for s in range(n_slab):
        hq_ref[s] = hq[:, s * LANES:(s + 1) * LANES]
        hkv_ref[s] = hkv[:, s * LANES:(s + 1) * LANES]

    def residue_major(h_ref, d):
        rows = tm // d
        return jnp.concatenate(
            [jnp.concatenate([h_ref[s, pl.ds(r, rows, stride=d), :] for s in range(n_slab)],
                             axis=1) for r in range(d)], axis=0).astype(BF16)

    for g, d in enumerate(dils):
        if d == 1:
            a_q, a_kv = hq.astype(BF16), hkv.astype(BF16)
        else:
            a_q, a_kv = residue_major(hq_ref, d), residue_major(hkv_ref, d)
        q = (_dot(a_q, wq_ref[:, g * gw:(g + 1) * gw]) * scale).astype(BF16)
        kv = _dot(a_kv, wkv_ref[:, g * 2 * gw:(g + 1) * 2 * gw]).astype(BF16)
        rows = tm // d
        for r in range(d):
            outs[g][0, r, :, :gw] = q[r * rows:(r + 1) * rows]
            outs[g][0, r, :, gw:] = kv[r * rows:(r + 1) * rows]


def b_proj(x, g_q, g_kv, w_q, w_kv, *, batch, tm, dils, scale):
    T, D = x.shape
    S = T // batch
    tps = S // tm
    gw = w_q.shape[1] // len(dils)
    resident = dict(pipeline_mode=pl.Buffered(1))
    return pl.pallas_call(
        functools.partial(_b_proj_kernel, dils=dils, scale=scale),
        grid=(T // tm,),
        in_specs=[pl.BlockSpec((tm, D), lambda i: (i, 0)),
                  pl.BlockSpec((1, D), lambda i: (0, 0)),
                  pl.BlockSpec((1, D), lambda i: (0, 0)),
                  pl.BlockSpec(w_q.shape, lambda i: (0, 0), **resident),
                  pl.BlockSpec(w_kv.shape, lambda i: (0, 0), **resident)],
        out_specs=[pl.BlockSpec((1, d, tm // d, 3 * gw), lambda i: (i // tps, 0, i % tps, 0))
                   for d in dils],
        out_shape=[jax.ShapeDtypeStruct((batch, d, S // d, 3 * gw), BF16) for d in dils],
        scratch_shapes=[pltpu.VMEM((D // LANES, tm, LANES), F32),
                        pltpu.VMEM((D // LANES, tm, LANES), F32)],
        compiler_params=_params("parallel"),
        name="b_proj",
    )(x, g_q, g_kv, w_q, w_kv)


def _split3(x):
    hi = x.astype(BF16)
    rem = x - hi.astype(F32)
    mid = rem.astype(BF16)
    return hi, mid, (rem - mid.astype(F32)).astype(BF16)


def _fox_proj_kernel(x_ref, g_ref, wqk_ref, wvt_ref, wf_ref, bf_ref, eq_ref, ek_ref, cq_ref,
                     ck_ref, qk_ref, vt_ref, qaug_ref, kaug_ref, carry_ref,
                     *, q_cols, scale, log_base, n_heads, tiles_per_seq):
    tm = x_ref.shape[0]

    @pl.when(pl.program_id(0) == 0)
    def _():
        carry_ref[...] = jnp.zeros_like(carry_ref)

    h = _rms_norm(x_ref[...], g_ref[...]).astype(BF16)

    z = _dot(h, wf_ref[...]) + bf_ref[...]
    logf = jnp.minimum(z, 0.0) - jnp.log(1.0 + jnp.exp(-jnp.abs(z)))
    r = lax.broadcasted_iota(jnp.int32, (tm, tm), 0)
    c = lax.broadcasted_iota(jnp.int32, (tm, tm), 1)
    tri = jnp.where(r >= c, 1.0, 0.0).astype(BF16)
    p_hi, p_mid, p_lo = _split3(logf)
    first = (pl.program_id(0) % tiles_per_seq) == 0
    prev = jnp.where(first, 0.0, carry_ref[SUBLANES - 1:SUBLANES, :])
    cum = _dot(tri, p_hi) + _dot(tri, p_mid) + _dot(tri, p_lo) + prev
    carry_ref[...] = cum[tm - SUBLANES:]
    hi, mid, lo = (t.astype(F32) for t in _split3(cum * log_base))
    lane = lax.broadcasted_iota(jnp.int32, cum.shape, 1)
    packed = jnp.where(lane < n_heads, hi,
                       jnp.where(lane < 2 * n_heads, pltpu.roll(mid, n_heads, axis=1),
                                 pltpu.roll(lo, 2 * n_heads, axis=1))).astype(BF16)
    qaug_ref[...] = (_dot(packed, eq_ref[...]) + cq_ref[...]).astype(BF16)
    kaug_ref[...] = (_dot(packed, ek_ref[...]) + ck_ref[...]).astype(BF16)

    qk_ref[:, :q_cols] = (_dot(h, wqk_ref[:, :q_cols]) * scale).astype(BF16)
    qk_ref[:, q_cols:] = _dot(h, wqk_ref[:, q_cols:]).astype(BF16)
    vt_ref[...] = _dot_nt(wvt_ref[...], h).astype(BF16)


def _aug_placement(n_heads):
    assert 3 * n_heads <= LANES
    width = n_heads // 2 * LANES
    eq = np.zeros((LANES, width), np.float32)
    ek = np.zeros((LANES, width), np.float32)
    cq = np.zeros((1, width), np.float32)
    ck = np.zeros((1, width), np.float32)
    for hd in range(n_heads):
        base = hd // 2 * LANES + (HEAD_DIM if hd % 2 == 0 else 0)
        for part in range(3):
            ek[part * n_heads + hd, base + part] = 1.0
            eq[part * n_heads + hd, base + 3 + part] = 1.0
            cq[0, base + part] = -1.0
            ck[0, base + 3 + part] = 1.0
    return (jnp.asarray(eq, BF16), jnp.asarray(ek, BF16), jnp.asarray(cq), jnp.asarray(ck))


def fox_proj(x, g, w_qk, w_vt, w_f, b_f, *, tm, seq_len, n_heads, q_cols, scale, log_base):
    T, D = x.shape
    n_qk = w_qk.shape[1]
    n_v = w_vt.shape[0]
    aug_w = n_heads // 2 * LANES
    resident = dict(pipeline_mode=pl.Buffered(1))
    const = lambda shape: pl.BlockSpec(shape, lambda i: (0, 0))
    return pl.pallas_call(
        functools.partial(_fox_proj_kernel, q_cols=q_cols, scale=scale, log_base=log_base,
                          n_heads=n_heads,
                          tiles_per_seq=seq_len // tm),
        grid=(T // tm,),
        in_specs=[pl.BlockSpec((tm, D), lambda i: (i, 0)),
                  const((1, D)),
                  pl.BlockSpec((D, n_qk), lambda i: (0, 0), **resident),
                  pl.BlockSpec((n_v, D), lambda i: (0, 0), **resident),
                  const((D, LANES)), const((1, LANES)),
                  const((LANES, aug_w)), const((LANES, aug_w)),
                  const((1, aug_w)), const((1, aug_w))],
        out_specs=[pl.BlockSpec((tm, n_qk), lambda i: (i, 0)),
                   pl.BlockSpec((n_v, tm), lambda i: (0, i)),
                   pl.BlockSpec((tm, aug_w), lambda i: (i, 0)),
                   pl.BlockSpec((tm, aug_w), lambda i: (i, 0))],
        out_shape=[jax.ShapeDtypeStruct((T, n_qk), BF16),
                   jax.ShapeDtypeStruct((n_v, T), BF16),
                   jax.ShapeDtypeStruct((T, aug_w), BF16),
                   jax.ShapeDtypeStruct((T, aug_w), BF16)],
        scratch_shapes=[pltpu.VMEM((SUBLANES, LANES), F32)],
        compiler_params=_params("arbitrary"),
        name="fox_proj",
    )(x, g, w_qk, w_vt, w_f, b_f, *_aug_placement(n_heads))


FOX_TQ = 4096
FOX_TK = 256
FOX_DEN_ROWS = 16
LOG2E = 1.4426950408889634


def _row_groups(x):
    return [x[r * SUBLANES:(r + 1) * SUBLANES] for r in range(x.shape[0] // SUBLANES)]


def _tree(op, xs):
    while len(xs) > 1:
        xs = [op(xs[a], xs[a + 1]) for a in range(0, len(xs) - 1, 2)] + (
            [xs[-1]] if len(xs) % 2 else [])
    return xs[0]


def _sublane_allreduce(op, x):
    for shift in (4, 2, 1):
        x = op(x, pltpu.roll(x, shift, axis=0))
    return x


def _tile_rows(x, reps):
    return jnp.concatenate([x] * reps, axis=0)


def _fox_kernel(q_ref, k_ref, vt_ref, qaug_ref, kaug_ref, o_ref,
                qa_ref, ka_ref, m_ref, acc_ref, st_ref, p_ref, alpha_ref):
    S = q_ref.shape[1]
    tq, tk = min(FOX_TQ, S), FOX_TK
    lane = lax.broadcasted_iota(jnp.int32, (tq, LANES), 1)

    def augment(c, carry):
        rows = pl.ds(pl.multiple_of(c * tq, tq), tq)
        for hh in range(2):
            in_head = (lane < HEAD_DIM) if hh == 0 else (lane >= HEAD_DIM)
            qa_ref[hh, rows, :] = jnp.where(in_head, q_ref[0, rows, :], qaug_ref[0, rows, :])
            ka_ref[hh, rows, :] = jnp.where(in_head, k_ref[0, rows, :], kaug_ref[0, rows, :])
        return carry

    lax.fori_loop(0, S // tq, augment, 0)
    acc_rows = acc_ref.shape[1]
    ones = jnp.ones((acc_rows - HEAD_DIM, tk), BF16)


    def scores(i, j, lo=0):
        q0 = pl.multiple_of(i * tq + lo, tk)
        k0 = pl.multiple_of(j * tk, tk)
        return [_dot_nt(ka_ref[hh, pl.ds(k0, tk), :], qa_ref[hh, pl.ds(q0, tq - lo), :])
                for hh in range(2)]

    def weighted_values(j, par, lo=0):
        k0 = pl.multiple_of(j * tk, tk)
        return [_dot(jnp.concatenate(
            [vt_ref[hh * HEAD_DIM:(hh + 1) * HEAD_DIM, pl.ds(k0, tk)], ones], axis=0),
            p_ref[par, hh, :, lo:]) for hh in range(2)]

    def accumulate(pv, lo=0):
        for hh in range(2):
            acc_ref[hh, :, lo:] = (
                acc_ref[hh, :, lo:] * _tile_rows(alpha_ref[hh, :, lo:], acc_rows // SUBLANES)
                + pv[hh])

    def step(i, j, par, lo=0, prev_lo=0, diagonal=False, next_lo=None):
        nxt = scores(i, j + 1, next_lo) if next_lo is not None else None
        pv = weighted_values(jnp.maximum(j - 1, 0), 1 - par, prev_lo)
        alphas = []
        for hh in range(2):
            st = st_ref[par, hh, :, lo:]
            if diagonal:
                key = lax.broadcasted_iota(jnp.int32, st.shape, 0)
                qry = lax.broadcasted_iota(jnp.int32, st.shape, 1)
                st = jnp.where(key <= qry, st, NEG_BIG)
            m_old = m_ref[hh, :, lo:]
            m_new = jnp.maximum(m_old, _sublane_allreduce(
                jnp.maximum, _tree(jnp.maximum, _row_groups(st))))
            alphas.append(jnp.exp2(m_old - m_new))
            m_ref[hh, :, lo:] = m_new
            p_ref[par, hh, :, lo:] = jnp.exp2(
                st - _tile_rows(m_new, tk // SUBLANES)).astype(BF16)
        if nxt is not None:
            for hh in range(2):
                st_ref[1 - par, hh, :, next_lo:] = nxt[hh]
        accumulate(pv, prev_lo)
        for hh in range(2):
            alpha_ref[hh, :, lo:] = alphas[hh]

    per_q = tq // tk
    assert per_q % 2 == 0

    def kv_body(i, jj, carry):
        for dj in range(per_q):
            step(i, jj * per_q + dj, dj % 2, next_lo=0)
        return carry

    def q_tile(i, carry):
        m_ref[...] = jnp.full(m_ref.shape, NEG_BIG, F32)
        acc_ref[...] = jnp.zeros(acc_ref.shape, F32)
        p_ref[...] = jnp.zeros(p_ref.shape, BF16)
        alpha_ref[...] = jnp.ones(alpha_ref.shape, F32)
        first = scores(i, 0)
        for hh in range(2):
            st_ref[0, hh] = first[hh]
        lax.fori_loop(0, i, functools.partial(kv_body, i), 0)
        for dj in range(per_q):
            step(i, i * per_q + dj, dj % 2, lo=dj * tk, prev_lo=max(dj - 1, 0) * tk, diagonal=True,
                 next_lo=(dj + 1) * tk if dj + 1 < per_q else None)
        last_lo = (per_q - 1) * tk
        accumulate(weighted_values((i + 1) * per_q - 1, (per_q - 1) % 2, last_lo), last_lo)
        outs = []
        for hh in range(2):
            l = acc_ref[hh, HEAD_DIM:HEAD_DIM + SUBLANES]
            outs.append(acc_ref[hh, :HEAD_DIM] / _tile_rows(l, HEAD_DIM // SUBLANES))
        ot = jnp.concatenate(outs, axis=0)
        o_ref[0, pl.ds(pl.multiple_of(i * tq, tq), tq), :] = ot.T.astype(o_ref.dtype)
        return carry

    lax.fori_loop(0, S // tq, q_tile, 0)


def fox_attention(qk, vt, qaug, kaug):
    B, S, W2 = qk.shape
    npair = W2 // 2 // LANES
    t = min(FOX_TQ, S)
    return pl.pallas_call(
        _fox_kernel,
        grid=(B, npair),
        in_specs=[
            pl.BlockSpec((1, S, LANES), lambda b, h: (b, 0, h)),
            pl.BlockSpec((1, S, LANES), lambda b, h: (b, 0, npair + h)),
            pl.BlockSpec((LANES, S), lambda b, h: (h, b)),
            pl.BlockSpec((1, S, LANES), lambda b, h: (b, 0, h)),
            pl.BlockSpec((1, S, LANES), lambda b, h: (b, 0, h)),
        ],
        out_specs=pl.BlockSpec((1, S, LANES), lambda b, h: (b, 0, h)),
        out_shape=jax.ShapeDtypeStruct((B, S, npair * LANES), BF16),
        scratch_shapes=[pltpu.VMEM((2, S, LANES), BF16),
                        pltpu.VMEM((2, S, LANES), BF16),
                        pltpu.VMEM((2, SUBLANES, t), F32),
                        pltpu.VMEM((2, HEAD_DIM + FOX_DEN_ROWS, t), F32),
                        pltpu.VMEM((2, 2, FOX_TK, t), F32),
                        pltpu.VMEM((2, 2, FOX_TK, t), BF16),
                        pltpu.VMEM((2, SUBLANES, t), F32)],
        compiler_params=_params("parallel", "parallel"),
        name="fox_attention",
    )(qk, qk, vt, qaug, kaug)


FFN_CHUNK = 256


def _shift_rows(u, prev, shift):
    rolled = pltpu.roll(u, shift, axis=0)
    head = jnp.concatenate([prev, u[:SUBLANES]], axis=0)
    top = pltpu.roll(head, shift, axis=0)[SUBLANES:]
    return jnp.concatenate([top, rolled[SUBLANES:]], axis=0)


def _merged_groups(refs):
    wide = lambda ref: jnp.concatenate([ref[0, hp] for hp in range(ref.shape[1])], axis=1)
    n = len(refs) // 2
    outs, lses = [wide(r) for r in refs[:n]], [wide(r) for r in refs[n:]]
    m = functools.reduce(jnp.maximum, lses)
    es = [jnp.exp2(l - m) for l in lses]
    return sum(e * o for e, o in zip(es, outs)) / sum(es)


def _ffn_kernel(x_ref, *rest, n_mix, d_ff, tiles_per_seq, final_norm):
    mix_refs, rest = rest[:n_mix], rest[n_mix:]
    wo_ref, g_ref, wup_ref, cw_ref, cb_ref, wd_ref = rest[:6]
    rest = rest[6:]
    if final_norm:
        gf_ref, o_ref, h_ref, act_ref, carry_ref = rest
    else:
        o_ref, h_ref, act_ref, carry_ref = rest
    tm = x_ref.shape[0]
    first = (pl.program_id(0) % tiles_per_seq) == 0

    @pl.when(pl.program_id(0) == 0)
    def _():
        carry_ref[...] = jnp.zeros_like(carry_ref)

    mixed = mix_refs[0][...] if n_mix == 1 else _merged_groups(mix_refs).astype(BF16)
    x = x_ref[...] + _dot(mixed, wo_ref[...])
    h_ref[...] = _rms_norm(x, g_ref[...]).astype(BF16)

    def conv_branch(col):
        sl = slice(col, col + FFN_CHUNK)
        u = _dot(h_ref[...], wup_ref[:, sl])
        prev = jnp.where(first, 0.0, carry_ref[:, sl])
        carry_ref[:, sl] = u[tm - SUBLANES:]
        u1 = _shift_rows(u, prev, 1)
        u2 = _shift_rows(u, prev, 2)
        return (cw_ref[0:1, sl] * u2 + cw_ref[1:2, sl] * u1 + cw_ref[2:3, sl] * u
                + cb_ref[:, sl])

    for c in range(d_ff // FFN_CHUNK):
        a = conv_branch(c * FFN_CHUNK)
        gate = conv_branch(d_ff + c * FFN_CHUNK)
        act = gate * (1.0 / (1.0 + jnp.exp(-gate))) * a
        act_ref[:, c * FFN_CHUNK:(c + 1) * FFN_CHUNK] = act.astype(BF16)

    y = x + _dot(act_ref[...], wd_ref[...])
    if final_norm:
        y = _rms_norm(y, gf_ref[...])
    o_ref[...] = y


def mixer_out_conv_ffn(x, mixed, w_out, g, w_up, conv_w, conv_b, w_down, *, seq_len, tm,
                       final_g=None):
    T, D = x.shape
    d_ff = w_down.shape[0]
    tps = seq_len // tm
    resident = dict(pipeline_mode=pl.Buffered(1))
    if len(mixed) == 1:
        mix_specs = [pl.BlockSpec((tm, mixed[0].shape[1]), lambda i: (i, 0))]
    else:
        npair = mixed[0].shape[1]
        mix_specs = [pl.BlockSpec((1, npair, tm, LANES),
                                  lambda i: (i // tps, 0, i % tps, 0))] * len(mixed)
    in_specs = [pl.BlockSpec((tm, D), lambda i: (i, 0))] + mix_specs + [
        pl.BlockSpec(w_out.shape, lambda i: (0, 0), **resident),
        pl.BlockSpec((1, D), lambda i: (0, 0)),
        pl.BlockSpec((D, 2 * d_ff), lambda i: (0, 0), **resident),
        pl.BlockSpec((CONV_WIDTH, 2 * d_ff), lambda i: (0, 0)),
        pl.BlockSpec((1, 2 * d_ff), lambda i: (0, 0)),
        pl.BlockSpec((d_ff, D), lambda i: (0, 0), **resident),
    ]
    args = [x, *mixed, w_out, g, w_up, conv_w, conv_b, w_down]
    if final_g is not None:
        in_specs.append(pl.BlockSpec((1, D), lambda i: (0, 0)))
        args.append(final_g)
    kern = functools.partial(_ffn_kernel, n_mix=len(mixed), d_ff=d_ff, tiles_per_seq=tps,
                             final_norm=final_g is not None)
    return pl.pallas_call(
        kern,
        grid=(T // tm,),
        in_specs=in_specs,
        out_specs=pl.BlockSpec((tm, D), lambda i: (i, 0)),
        out_shape=jax.ShapeDtypeStruct((T, D), F32),
        scratch_shapes=[pltpu.VMEM((tm, D), BF16),
                        pltpu.VMEM((tm, d_ff), BF16),
                        pltpu.VMEM((SUBLANES, 2 * d_ff), F32)],
        compiler_params=_params("arbitrary"),
        name="conv_ffn_final" if final_g is not None else "conv_ffn",
    )(*args)


DILATED_TILES_PER_TRIP = 8


def _dilated_kernel(q_ref, k_ref, v_ref, kp_ref, vp_ref, bias_ref, o_ref, lse_ref,
                    *, dil, rows):
    W = B_WINDOW_STEPS
    first_chunk = pl.program_id(1) == 0
    lane = lax.broadcasted_iota(jnp.int32, (W, LANES), 1)
    col = lax.broadcasted_iota(jnp.int32, (W, 2 * W), 1)
    npair = q_ref.shape[3] // LANES
    tiles = rows // W

    def tile(tt, carry):
        r = tt // tiles
        n = tt % tiles
        r0 = pl.multiple_of(n * W, W)
        prev = pl.ds(pl.multiple_of(jnp.maximum(n - 1, 0) * W, W), W)
        dst = pl.ds(r0, W) if dil == 1 else pl.ds(r0 * dil + r, W, stride=dil)
        no_prev = jnp.logical_and(first_chunk, n == 0)
        kill_prev = jnp.logical_and(no_prev, col < W)
        for hp in range(npair):
            ls = slice(hp * LANES, (hp + 1) * LANES)
            q = q_ref[0, r, pl.ds(r0, W), ls]
            kk = jnp.concatenate([jnp.where(n == 0, kp_ref[0, r, :, ls], k_ref[0, r, prev, ls]),
                                  k_ref[0, r, pl.ds(r0, W), ls]], axis=0)
            vv = jnp.concatenate([jnp.where(n == 0, vp_ref[0, r, :, ls], v_ref[0, r, prev, ls]),
                                  v_ref[0, r, pl.ds(r0, W), ls]], axis=0)
            zero = jnp.zeros_like(q)
            o_h, lse_h = [], []
            for hh in range(2):
                in_head = (lane < HEAD_DIM) if hh == 0 else (lane >= HEAD_DIM)
                qm = jnp.where(in_head, q, zero)
                bias = jnp.where(kill_prev, NEG_BIG, bias_ref[2 * hp + hh])
                s = _dot_nt(qm, kk) + bias
                m = jnp.max(s, axis=-1, keepdims=True)
                p = jnp.exp2(s - m)
                l = jnp.sum(p, axis=-1, keepdims=True)
                o_h.append(_dot(p.astype(BF16), vv) / l)
                lse_h.append(m + jnp.log2(l))
            o_ref[0, hp, dst, :] = jnp.where(lane < HEAD_DIM, o_h[0], o_h[1])
            lse_ref[0, hp, dst, :] = jnp.where(lane < HEAD_DIM, lse_h[0], lse_h[1])
        return carry

    def tile_group(tg, carry):
        for u in range(DILATED_TILES_PER_TRIP):
            tile(tg * DILATED_TILES_PER_TRIP + u, carry)
        return carry

    lax.fori_loop(0, dil * tiles // DILATED_TILES_PER_TRIP, tile_group, 0)


def dilated_attention(qkv, bias, *, dil, rows):
    B, _, L, w3 = qkv.shape
    gw = w3 // 3
    W = B_WINDOW_STEPS
    npair = gw // LANES
    tiles = rows // W
    span = rows * dil

    def prev_tile(c):
        return jnp.maximum(c * tiles - 1, 0)

    out = jax.ShapeDtypeStruct((B, npair, L * dil, LANES), F32)
    out_spec = pl.BlockSpec((1, npair, span, LANES), lambda b, c: (b, 0, c, 0))
    return pl.pallas_call(
        functools.partial(_dilated_kernel, dil=dil, rows=rows),
        grid=(B, L // rows),
        in_specs=[
            pl.BlockSpec((1, dil, rows, gw), lambda b, c: (b, 0, c, 0)),
            pl.BlockSpec((1, dil, rows, gw), lambda b, c: (b, 0, c, 1)),
            pl.BlockSpec((1, dil, rows, gw), lambda b, c: (b, 0, c, 2)),
            pl.BlockSpec((1, dil, W, gw), lambda b, c: (b, 0, prev_tile(c), 1)),
            pl.BlockSpec((1, dil, W, gw), lambda b, c: (b, 0, prev_tile(c), 2)),
            pl.BlockSpec(bias.shape, lambda b, c: (0, 0, 0)),
        ],
        out_specs=[out_spec, out_spec],
        out_shape=[out, out],
        compiler_params=_params("parallel", "arbitrary"),
        name=f"dilated_attention_d{dil}",
    )(qkv, qkv, qkv, qkv, qkv, bias)


def _dilated_bias(group, dil):
    W = B_WINDOW_STEPS
    n_heads = len(B_GROUP_DILATIONS) * B_HEADS_PER_GROUP
    idx = jnp.arange(1, n_heads + 1, dtype=F32)
    slopes = jnp.exp2(-8.0 * idx / n_heads)[group * B_HEADS_PER_GROUP:
                                             (group + 1) * B_HEADS_PER_GROUP]
    dist = jnp.arange(W)[:, None] + W - jnp.arange(2 * W)[None, :]
    band = (dist >= 0) & (dist <= W)
    bias = -slopes[:, None, None] * (dil * dist).astype(F32)
    return jnp.where(band[None], bias * LOG2E, NEG_BIG)


def _tiles(T, S):
    return dict(
        row_tm=min(512, S),
        dil_span={d: min(S, max(1024, B_WINDOW_STEPS * d)) for d in B_GROUP_DILATIONS},
    )


def kernel(x, a_w_in, a_b_f, a_w_out, b_w_q, b_w_out, kv_norm_g, w_kv, mix_norm_g,
           ffn_norm_g, ffn_w_up, ffn_conv_w, ffn_conv_b, ffn_w_down, final_norm_g):
    B, S, D = x.shape
    T = B * S
    t = _tiles(T, S)
    depth = mix_norm_g.shape[0]
    n_a = a_w_in.shape[0]
    qk_scale = HEAD_DIM ** -0.5
    row = lambda v: v.reshape(1, -1).astype(F32)

    xs = x.reshape(T, D)
    kv_w = None
    for layer in range(depth):
        if layer < n_a:
            hw = A_HEADS * HEAD_DIM
            w_in = a_w_in[layer]
            w_qk = w_in[:, :2 * hw].astype(BF16)
            w_vt = w_in[:, 2 * hw:3 * hw].T.astype(BF16)
            w_f = jnp.pad(w_in[:, 3 * hw:], ((0, 0), (0, LANES - A_HEADS))).astype(BF16)
            b_f = jnp.pad(a_b_f[layer].astype(F32), (0, LANES - A_HEADS)).reshape(1, LANES)
            qk, vt, qaug, kaug = fox_proj(
                xs, row(mix_norm_g[layer]), w_qk, w_vt, w_f, b_f, tm=t["row_tm"], seq_len=S,
                n_heads=A_HEADS, q_cols=hw, scale=qk_scale * LOG2E, log_base=LOG2E)
            o = fox_attention(qk.reshape(B, S, 2 * hw), vt, qaug.reshape(B, S, hw),
                              kaug.reshape(B, S, hw))
            mixed, w_out = [o.reshape(T, -1)], a_w_out[layer].astype(BF16)
        else:
            bl = layer - n_a
            n_g = len(B_GROUP_DILATIONS)
            if kv_w is None:
                gw = w_kv.shape[1] // (2 * n_g)
                kv_w = w_kv.reshape(D, 2, n_g, gw).transpose(0, 2, 1, 3).reshape(D, -1).astype(BF16)
            qkv = b_proj(xs, row(mix_norm_g[layer]), row(kv_norm_g), b_w_q[bl].astype(BF16), kv_w,
                         batch=B, tm=t["row_tm"], dils=B_GROUP_DILATIONS, scale=qk_scale * LOG2E)
            outs, lses = [], []
            for g, dil in enumerate(B_GROUP_DILATIONS):
                o_g, lse_g = dilated_attention(qkv[g], _dilated_bias(g, dil), dil=dil,
                                               rows=t["dil_span"][dil] // dil)
                outs.append(o_g)
                lses.append(lse_g)
            mixed, w_out = outs + lses, b_w_out[bl].astype(BF16)
        last = layer == depth - 1
        xs = mixer_out_conv_ffn(
            xs, mixed, w_out, row(ffn_norm_g[layer]), ffn_w_up[layer].astype(BF16),
            ffn_conv_w[layer].astype(F32), row(ffn_conv_b[layer]),
            ffn_w_down[layer].astype(BF16), seq_len=S, tm=t["row_tm"],
            final_g=row(final_norm_g) if last else None)
    return xs.reshape(B, S, D)
```

```python
import functools

import numpy as np
import jax
import jax.numpy as jnp
from jax import lax
from jax.experimental import pallas as pl
from jax.experimental.pallas import tpu as pltpu

F32 = jnp.float32
BF16 = jnp.bfloat16

RMS_EPS = 1e-6
HEAD_DIM = 64
LANES = 128
SUBLANES = 8
A_HEADS = 16
B_GROUP_DILATIONS = (1, 4, 16)
B_HEADS_PER_GROUP = 8
B_WINDOW_STEPS = 128
CONV_WIDTH = 3
NEG_BIG = -1e30
VMEM_LIMIT_BYTES = 56 * 1024 * 1024


def _params(*semantics):
    return pltpu.CompilerParams(dimension_semantics=semantics,
                                vmem_limit_bytes=VMEM_LIMIT_BYTES)


def _rms_norm(x, g):
    y = x * lax.rsqrt(jnp.mean(x * x, axis=-1, keepdims=True) + RMS_EPS)
    return y * g


def _dot(a, b):
    return jnp.dot(a, b, preferred_element_type=F32)


def _dot_nt(a, b):
    return lax.dot_general(a, b, (((1,), (1,)), ((), ())), preferred_element_type=F32)


def _b_proj_kernel(x_ref, gq_ref, gkv_ref, wq_ref, wkv_ref, *rest, dils, scale):
    outs = rest[:len(dils)]
    hq_ref, hkv_ref = rest[len(dils):]
    tm, D = x_ref.shape
    gw = wq_ref.shape[1] // len(dils)
    n_slab = D // LANES
    x = x_ref[...]
    xn = x * lax.rsqrt(jnp.mean(x * x, axis=-1, keepdims=True) + RMS_EPS)
    hq = xn * gq_ref[...]
    hkv = xn * gkv_ref[...]
    for s in range(n_slab):
        hq_ref[s] = hq[:, s * LANES:(s + 1) * LANES]
        hkv_ref[s] = hkv[:, s * LANES:(s + 1) * LANES]

    def residue_major(h_ref, d):
        rows = tm // d
        return jnp.concatenate(
            [jnp.concatenate([h_ref[s, pl.ds(r, rows, stride=d), :] for s in range(n_slab)],
                             axis=1) for r in range(d)], axis=0).astype(BF16)

    for g, d in enumerate(dils):
        if d == 1:
            a_q, a_kv = hq.astype(BF16), hkv.astype(BF16)
        else:
            a_q, a_kv = residue_major(hq_ref, d), residue_major(hkv_ref, d)
        q = (_dot(a_q, wq_ref[:, g * gw:(g + 1) * gw]) * scale).astype(BF16)
        n_g = len(dils)
        k = _dot(a_kv, wkv_ref[:, g * gw:(g + 1) * gw]).astype(BF16)
        v = _dot(a_kv, wkv_ref[:, (n_g + g) * gw:(n_g + g + 1) * gw]).astype(BF16)
        rows = tm // d
        for r in range(d):
            outs[g][0, r, :, :gw] = q[r * rows:(r + 1) * rows]
            outs[g][0, r, :, gw:2 * gw] = k[r * rows:(r + 1) * rows]
            outs[g][0, r, :, 2 * gw:] = v[r * rows:(r + 1) * rows]


def b_proj(x, g_q, g_kv, w_q, w_kv, *, batch, tm, dils, scale):
    T, D = x.shape
    S = T // batch
    tps = S // tm
    gw = w_q.shape[1] // len(dils)
    resident = dict(pipeline_mode=pl.Buffered(1))
    return pl.pallas_call(
        functools.partial(_b_proj_kernel, dils=dils, scale=scale),
        grid=(T // tm,),
        in_specs=[pl.BlockSpec((tm, D), lambda i: (i, 0)),
                  pl.BlockSpec((1, D), lambda i: (0, 0)),
                  pl.BlockSpec((1, D), lambda i: (0, 0)),
                  pl.BlockSpec(w_q.shape, lambda i: (0, 0), **resident),
                  pl.BlockSpec(w_kv.shape, lambda i: (0, 0), **resident)],
        out_specs=[pl.BlockSpec((1, d, tm // d, 3 * gw), lambda i: (i // tps, 0, i % tps, 0))
                   for d in dils],
        out_shape=[jax.ShapeDtypeStruct((batch, d, S // d, 3 * gw), BF16) for d in dils],
        scratch_shapes=[pltpu.VMEM((D // LANES, tm, LANES), F32),
                        pltpu.VMEM((D // LANES, tm, LANES), F32)],
        compiler_params=_params("parallel"),
        name="b_proj",
    )(x, g_q, g_kv, w_q, w_kv)


def _split3(x):
    hi = x.astype(BF16)
    rem = x - hi.astype(F32)
    mid = rem.astype(BF16)
    return hi, mid, (rem - mid.astype(F32)).astype(BF16)


def _fox_proj_kernel(x_ref, g_ref, wqk_ref, wvt_ref, wf_ref, bf_ref, eq_ref, ek_ref, cq_ref,
                     ck_ref, qk_ref, vt_ref, qaug_ref, kaug_ref, carry_ref,
                     *, q_cols, scale, log_base, n_heads, tiles_per_seq):
    tm = x_ref.shape[0]

    @pl.when(pl.program_id(0) == 0)
    def _():
        carry_ref[...] = jnp.zeros_like(carry_ref)

    h = _rms_norm(x_ref[...], g_ref[...]).astype(BF16)

    z = _dot(h, wf_ref[...]) + bf_ref[...]
    logf = jnp.minimum(z, 0.0) - jnp.log(1.0 + jnp.exp(-jnp.abs(z)))
    r = lax.broadcasted_iota(jnp.int32, (tm, tm), 0)
    c = lax.broadcasted_iota(jnp.int32, (tm, tm), 1)
    tri = jnp.where(r >= c, 1.0, 0.0).astype(BF16)
    p_hi, p_mid, p_lo = _split3(logf)
    first = (pl.program_id(0) % tiles_per_seq) == 0
    prev = jnp.where(first, 0.0, carry_ref[SUBLANES - 1:SUBLANES, :])
    cum = _dot(tri, p_hi) + _dot(tri, p_mid) + _dot(tri, p_lo) + prev
    carry_ref[...] = cum[tm - SUBLANES:]
    hi, mid, lo = (t.astype(F32) for t in _split3(cum * log_base))
    lane = lax.broadcasted_iota(jnp.int32, cum.shape, 1)
    packed = jnp.where(lane < n_heads, hi,
                       jnp.where(lane < 2 * n_heads, pltpu.roll(mid, n_heads, axis=1),
                                 pltpu.roll(lo, 2 * n_heads, axis=1))).astype(BF16)
    qaug_ref[...] = (_dot(packed, eq_ref[...]) + cq_ref[...]).astype(BF16)
    kaug_ref[...] = (_dot(packed, ek_ref[...]) + ck_ref[...]).astype(BF16)

    qk_ref[:, :q_cols] = (_dot(h, wqk_ref[:, :q_cols]) * scale).astype(BF16)
    qk_ref[:, q_cols:] = _dot(h, wqk_ref[:, q_cols:]).astype(BF16)
    vt_ref[...] = _dot_nt(wvt_ref[...], h).astype(BF16)


def _aug_placement(n_heads):
    assert 3 * n_heads <= LANES
    width = n_heads // 2 * LANES
    eq = np.zeros((LANES, width), np.float32)
    ek = np.zeros((LANES, width), np.float32)
    cq = np.zeros((1, width), np.float32)
    ck = np.zeros((1, width), np.float32)
    for hd in range(n_heads):
        base = hd // 2 * LANES + (HEAD_DIM if hd % 2 == 0 else 0)
        for part in range(3):
            ek[part * n_heads + hd, base + part] = 1.0
            eq[part * n_heads + hd, base + 3 + part] = 1.0
            cq[0, base + part] = -1.0
            ck[0, base + 3 + part] = 1.0
    return (jnp.asarray(eq, BF16), jnp.asarray(ek, BF16), jnp.asarray(cq), jnp.asarray(ck))


def fox_proj(x, g, w_qk, w_vt, w_f, b_f, *, tm, seq_len, n_heads, q_cols, scale, log_base):
    T, D = x.shape
    n_qk = w_qk.shape[1]
    n_v = w_vt.shape[0]
    aug_w = n_heads // 2 * LANES
    resident = dict(pipeline_mode=pl.Buffered(1))
    const = lambda shape: pl.BlockSpec(shape, lambda i: (0, 0))
    return pl.pallas_call(
        functools.partial(_fox_proj_kernel, q_cols=q_cols, scale=scale, log_base=log_base,
                          n_heads=n_heads,
                          tiles_per_seq=seq_len // tm),
        grid=(T // tm,),
        in_specs=[pl.BlockSpec((tm, D), lambda i: (i, 0)),
                  const((1, D)),
                  pl.BlockSpec((D, n_qk), lambda i: (0, 0), **resident),
                  pl.BlockSpec((n_v, D), lambda i: (0, 0), **resident),
                  const((D, LANES)), const((1, LANES)),
                  const((LANES, aug_w)), const((LANES, aug_w)),
                  const((1, aug_w)), const((1, aug_w))],
        out_specs=[pl.BlockSpec((tm, n_qk), lambda i: (i, 0)),
                   pl.BlockSpec((n_v, tm), lambda i: (0, i)),
                   pl.BlockSpec((tm, aug_w), lambda i: (i, 0)),
                   pl.BlockSpec((tm, aug_w), lambda i: (i, 0))],
        out_shape=[jax.ShapeDtypeStruct((T, n_qk), BF16),
                   jax.ShapeDtypeStruct((n_v, T), BF16),
                   jax.ShapeDtypeStruct((T, aug_w), BF16),
                   jax.ShapeDtypeStruct((T, aug_w), BF16)],
        scratch_shapes=[pltpu.VMEM((SUBLANES, LANES), F32)],
        compiler_params=_params("arbitrary"),
        name="fox_proj",
    )(x, g, w_qk, w_vt, w_f, b_f, *_aug_placement(n_heads))


FOX_TQ = 4096
FOX_TK = 256
FOX_DEN_ROWS = 16
LOG2E = 1.4426950408889634


def _row_groups(x):
    return [x[r * SUBLANES:(r + 1) * SUBLANES] for r in range(x.shape[0] // SUBLANES)]


def _tree(op, xs):
    while len(xs) > 1:
        xs = [op(xs[a], xs[a + 1]) for a in range(0, len(xs) - 1, 2)] + (
            [xs[-1]] if len(xs) % 2 else [])
    return xs[0]


def _sublane_allreduce(op, x):
    for shift in (4, 2, 1):
        x = op(x, pltpu.roll(x, shift, axis=0))
    return x


def _tile_rows(x, reps):
    return jnp.concatenate([x] * reps, axis=0)


def _fox_kernel(q_ref, k_ref, vt_ref, qaug_ref, kaug_ref, o_ref,
                qa_ref, ka_ref, m_ref, acc_ref, st_ref, p_ref, alpha_ref):
    S = q_ref.shape[1]
    tq, tk = min(FOX_TQ, S), FOX_TK
    lane = lax.broadcasted_iota(jnp.int32, (tq, LANES), 1)

    def augment(c, carry):
        rows = pl.ds(pl.multiple_of(c * tq, tq), tq)
        for hh in range(2):
            in_head = (lane < HEAD_DIM) if hh == 0 else (lane >= HEAD_DIM)
            qa_ref[hh, rows, :] = jnp.where(in_head, q_ref[0, rows, :], qaug_ref[0, rows, :])
            ka_ref[hh, rows, :] = jnp.where(in_head, k_ref[0, rows, :], kaug_ref[0, rows, :])
        return carry

    lax.fori_loop(0, S // tq, augment, 0)
    acc_rows = acc_ref.shape[1]
    ones = jnp.ones((acc_rows - HEAD_DIM, tk), BF16)


    def scores(i, j, lo=0):
        q0 = pl.multiple_of(i * tq + lo, tk)
        k0 = pl.multiple_of(j * tk, tk)
        return [_dot_nt(ka_ref[hh, pl.ds(k0, tk), :], qa_ref[hh, pl.ds(q0, tq - lo), :])
                for hh in range(2)]

    def weighted_values(j, par, lo=0):
        k0 = pl.multiple_of(j * tk, tk)
        return [_dot(jnp.concatenate(
            [vt_ref[hh * HEAD_DIM:(hh + 1) * HEAD_DIM, pl.ds(k0, tk)], ones], axis=0),
            p_ref[par, hh, :, lo:]) for hh in range(2)]

    def accumulate(pv, lo=0):
        for hh in range(2):
            acc_ref[hh, :, lo:] = (
                acc_ref[hh, :, lo:] * _tile_rows(alpha_ref[hh, :, lo:], acc_rows // SUBLANES)
                + pv[hh])

    def step(i, j, par, lo=0, prev_lo=0, diagonal=False, next_lo=None):
        nxt = scores(i, j + 1, next_lo) if next_lo is not None else None
        pv = weighted_values(jnp.maximum(j - 1, 0), 1 - par, prev_lo)
        alphas = []
        for hh in range(2):
            st = st_ref[par, hh, :, lo:]
            if diagonal:
                key = lax.broadcasted_iota(jnp.int32, st.shape, 0)
                qry = lax.broadcasted_iota(jnp.int32, st.shape, 1)
                st = jnp.where(key <= qry, st, NEG_BIG)
            m_old = m_ref[hh, :, lo:]
            m_new = jnp.maximum(m_old, _sublane_allreduce(
                jnp.maximum, _tree(jnp.maximum, _row_groups(st))))
            alphas.append(jnp.exp2(m_old - m_new))
            m_ref[hh, :, lo:] = m_new
            p_ref[par, hh, :, lo:] = jnp.exp2(
                st - _tile_rows(m_new, tk // SUBLANES)).astype(BF16)
        if nxt is not None:
            for hh in range(2):
                st_ref[1 - par, hh, :, next_lo:] = nxt[hh]
        accumulate(pv, prev_lo)
        for hh in range(2):
            alpha_ref[hh, :, lo:] = alphas[hh]

    per_q = tq // tk
    assert per_q % 2 == 0

    def kv_body(i, jj, carry):
        for dj in range(per_q):
            step(i, jj * per_q + dj, dj % 2, next_lo=0)
        return carry

    def q_tile(i, carry):
        m_ref[...] = jnp.full(m_ref.shape, NEG_BIG, F32)
        acc_ref[...] = jnp.zeros(acc_ref.shape, F32)
        p_ref[...] = jnp.zeros(p_ref.shape, BF16)
        alpha_ref[...] = jnp.ones(alpha_ref.shape, F32)
        first = scores(i, 0)
        for hh in range(2):
            st_ref[0, hh] = first[hh]
        lax.fori_loop(0, i, functools.partial(kv_body, i), 0)
        for dj in range(per_q):
            step(i, i * per_q + dj, dj % 2, lo=dj * tk, prev_lo=max(dj - 1, 0) * tk, diagonal=True,
                 next_lo=(dj + 1) * tk if dj + 1 < per_q else None)
        last_lo = (per_q - 1) * tk
        accumulate(weighted_values((i + 1) * per_q - 1, (per_q - 1) % 2, last_lo), last_lo)
        outs = []
        for hh in range(2):
            l = acc_ref[hh, HEAD_DIM:HEAD_DIM + SUBLANES]
            outs.append(acc_ref[hh, :HEAD_DIM] / _tile_rows(l, HEAD_DIM // SUBLANES))
        ot = jnp.concatenate(outs, axis=0)
        o_ref[0, pl.ds(pl.multiple_of(i * tq, tq), tq), :] = ot.T.astype(o_ref.dtype)
        return carry

    lax.fori_loop(0, S // tq, q_tile, 0)


def fox_attention(qk, vt, qaug, kaug):
    B, S, W2 = qk.shape
    npair = W2 // 2 // LANES
    t = min(FOX_TQ, S)
    return pl.pallas_call(
        _fox_kernel,
        grid=(B, npair),
        in_specs=[
            pl.BlockSpec((1, S, LANES), lambda b, h: (b, 0, h)),
            pl.BlockSpec((1, S, LANES), lambda b, h: (b, 0, npair + h)),
            pl.BlockSpec((LANES, S), lambda b, h: (h, b)),
            pl.BlockSpec((1, S, LANES), lambda b, h: (b, 0, h)),
            pl.BlockSpec((1, S, LANES), lambda b, h: (b, 0, h)),
        ],
        out_specs=pl.BlockSpec((1, S, LANES), lambda b, h: (b, 0, h)),
        out_shape=jax.ShapeDtypeStruct((B, S, npair * LANES), BF16),
        scratch_shapes=[pltpu.VMEM((2, S, LANES), BF16),
                        pltpu.VMEM((2, S, LANES), BF16),
                        pltpu.VMEM((2, SUBLANES, t), F32),
                        pltpu.VMEM((2, HEAD_DIM + FOX_DEN_ROWS, t), F32),
                        pltpu.VMEM((2, 2, FOX_TK, t), F32),
                        pltpu.VMEM((2, 2, FOX_TK, t), BF16),
                        pltpu.VMEM((2, SUBLANES, t), F32)],
        compiler_params=_params("parallel", "parallel"),
        name="fox_attention",
    )(qk, qk, vt, qaug, kaug)


FFN_CHUNK = 256


def _shift_rows(u, prev, shift):
    rolled = pltpu.roll(u, shift, axis=0)
    head = jnp.concatenate([prev, u[:SUBLANES]], axis=0)
    top = pltpu.roll(head, shift, axis=0)[SUBLANES:]
    return jnp.concatenate([top, rolled[SUBLANES:]], axis=0)


def _merged_groups(refs):
    wide = lambda ref: jnp.concatenate([ref[0, hp] for hp in range(ref.shape[1])], axis=1)
    n = len(refs) // 2
    outs, lses = [wide(r) for r in refs[:n]], [wide(r) for r in refs[n:]]
    m = functools.reduce(jnp.maximum, lses)
    es = [jnp.exp2(l - m) for l in lses]
    return sum(e * o for e, o in zip(es, outs)) / sum(es)


def _ffn_kernel(x_ref, *rest, n_mix, d_ff, tiles_per_seq, final_norm):
    mix_refs, rest = rest[:n_mix], rest[n_mix:]
    wo_ref, g_ref, wup_ref, cw_ref, cb_ref, wd_ref = rest[:6]
    rest = rest[6:]
    if final_norm:
        gf_ref, o_ref, h_ref, act_ref, carry_ref = rest
    else:
        o_ref, h_ref, act_ref, carry_ref = rest
    tm = x_ref.shape[0]
    first = (pl.program_id(0) % tiles_per_seq) == 0

    @pl.when(pl.program_id(0) == 0)
    def _():
        carry_ref[...] = jnp.zeros_like(carry_ref)

    mixed = mix_refs[0][...] if n_mix == 1 else _merged_groups(mix_refs).astype(BF16)
    x = x_ref[...] + _dot(mixed, wo_ref[...])
    h_ref[...] = _rms_norm(x, g_ref[...]).astype(BF16)

    def conv_branch(col):
        sl = slice(col, col + FFN_CHUNK)
        u = _dot(h_ref[...], wup_ref[:, sl])
        prev = jnp.where(first, 0.0, carry_ref[:, sl])
        carry_ref[:, sl] = u[tm - SUBLANES:]
        u1 = _shift_rows(u, prev, 1)
        u2 = _shift_rows(u, prev, 2)
        return (cw_ref[0:1, sl] * u2 + cw_ref[1:2, sl] * u1 + cw_ref[2:3, sl] * u
                + cb_ref[:, sl])

    for c in range(d_ff // FFN_CHUNK):
        a = conv_branch(c * FFN_CHUNK)
        gate = conv_branch(d_ff + c * FFN_CHUNK)
        act = gate * (1.0 / (1.0 + jnp.exp(-gate))) * a
        act_ref[:, c * FFN_CHUNK:(c + 1) * FFN_CHUNK] = act.astype(BF16)

    y = x + _dot(act_ref[...], wd_ref[...])
    if final_norm:
        y = _rms_norm(y, gf_ref[...])
    o_ref[...] = y


def mixer_out_conv_ffn(x, mixed, w_out, g, w_up, conv_w, conv_b, w_down, *, seq_len, tm,
                       final_g=None):
    T, D = x.shape
    d_ff = w_down.shape[0]
    tps = seq_len // tm
    resident = dict(pipeline_mode=pl.Buffered(1))
    if len(mixed) == 1:
        mix_specs = [pl.BlockSpec((tm, mixed[0].shape[1]), lambda i: (i, 0))]
    else:
        npair = mixed[0].shape[1]
        mix_specs = [pl.BlockSpec((1, npair, tm, LANES),
                                  lambda i: (i // tps, 0, i % tps, 0))] * len(mixed)
    in_specs = [pl.BlockSpec((tm, D), lambda i: (i, 0))] + mix_specs + [
        pl.BlockSpec(w_out.shape, lambda i: (0, 0), **resident),
        pl.BlockSpec((1, D), lambda i: (0, 0)),
        pl.BlockSpec((D, 2 * d_ff), lambda i: (0, 0), **resident),
        pl.BlockSpec((CONV_WIDTH, 2 * d_ff), lambda i: (0, 0)),
        pl.BlockSpec((1, 2 * d_ff), lambda i: (0, 0)),
        pl.BlockSpec((d_ff, D), lambda i: (0, 0), **resident),
    ]
    args = [x, *mixed, w_out, g, w_up, conv_w, conv_b, w_down]
    if final_g is not None:
        in_specs.append(pl.BlockSpec((1, D), lambda i: (0, 0)))
        args.append(final_g)
    kern = functools.partial(_ffn_kernel, n_mix=len(mixed), d_ff=d_ff, tiles_per_seq=tps,
                             final_norm=final_g is not None)
    return pl.pallas_call(
        kern,
        grid=(T // tm,),
        in_specs=in_specs,
        out_specs=pl.BlockSpec((tm, D), lambda i: (i, 0)),
        out_shape=jax.ShapeDtypeStruct((T, D), F32),
        scratch_shapes=[pltpu.VMEM((tm, D), BF16),
                        pltpu.VMEM((tm, d_ff), BF16),
                        pltpu.VMEM((SUBLANES, 2 * d_ff), F32)],
        compiler_params=_params("arbitrary"),
        name="conv_ffn_final" if final_g is not None else "conv_ffn",
    )(*args)


DILATED_TILES_PER_TRIP = 8


def _dilated_kernel(q_ref, k_ref, v_ref, kp_ref, vp_ref, bias_ref, o_ref, lse_ref,
                    *, dil, rows):
    W = B_WINDOW_STEPS
    first_chunk = pl.program_id(1) == 0
    lane = lax.broadcasted_iota(jnp.int32, (W, LANES), 1)
    col = lax.broadcasted_iota(jnp.int32, (W, 2 * W), 1)
    npair = q_ref.shape[3] // LANES
    tiles = rows // W

    def tile(tt, carry):
        r = tt // tiles
        n = tt % tiles
        r0 = pl.multiple_of(n * W, W)
        prev = pl.ds(pl.multiple_of(jnp.maximum(n - 1, 0) * W, W), W)
        dst = pl.ds(r0, W) if dil == 1 else pl.ds(r0 * dil + r, W, stride=dil)
        no_prev = jnp.logical_and(first_chunk, n == 0)
        kill_prev = jnp.logical_and(no_prev, col < W)
        for hp in range(npair):
            ls = slice(hp * LANES, (hp + 1) * LANES)
            q = q_ref[0, r, pl.ds(r0, W), ls]
            kk = jnp.concatenate([jnp.where(n == 0, kp_ref[0, r, :, ls], k_ref[0, r, prev, ls]),
                                  k_ref[0, r, pl.ds(r0, W), ls]], axis=0)
            vv = jnp.concatenate([jnp.where(n == 0, vp_ref[0, r, :, ls], v_ref[0, r, prev, ls]),
                                  v_ref[0, r, pl.ds(r0, W), ls]], axis=0)
            zero = jnp.zeros_like(q)
            o_h, lse_h = [], []
            for hh in range(2):
                in_head = (lane < HEAD_DIM) if hh == 0 else (lane >= HEAD_DIM)
                qm = jnp.where(in_head, q, zero)
                bias = jnp.where(kill_prev, NEG_BIG, bias_ref[2 * hp + hh])
                s = _dot_nt(qm, kk) + bias
                m = jnp.max(s, axis=-1, keepdims=True)
                p = jnp.exp2(s - m)
                l = jnp.sum(p, axis=-1, keepdims=True)
                o_h.append(_dot(p.astype(BF16), vv) / l)
                lse_h.append(m + jnp.log2(l))
            o_ref[0, hp, dst, :] = jnp.where(lane < HEAD_DIM, o_h[0], o_h[1])
            lse_ref[0, hp, dst, :] = jnp.where(lane < HEAD_DIM, lse_h[0], lse_h[1])
        return carry

    def tile_group(tg, carry):
        for u in range(DILATED_TILES_PER_TRIP):
            tile(tg * DILATED_TILES_PER_TRIP + u, carry)
        return carry

    lax.fori_loop(0, dil * tiles // DILATED_TILES_PER_TRIP, tile_group, 0)


def dilated_attention(qkv, bias, *, dil, rows):
    B, _, L, w3 = qkv.shape
    gw = w3 // 3
    W = B_WINDOW_STEPS
    npair = gw // LANES
    tiles = rows // W
    span = rows * dil

    def prev_tile(c):
        return jnp.maximum(c * tiles - 1, 0)

    out = jax.ShapeDtypeStruct((B, npair, L * dil, LANES), F32)
    out_spec = pl.BlockSpec((1, npair, span, LANES), lambda b, c: (b, 0, c, 0))
    return pl.pallas_call(
        functools.partial(_dilated_kernel, dil=dil, rows=rows),
        grid=(B, L // rows),
        in_specs=[
            pl.BlockSpec((1, dil, rows, gw), lambda b, c: (b, 0, c, 0)),
            pl.BlockSpec((1, dil, rows, gw), lambda b, c: (b, 0, c, 1)),
            pl.BlockSpec((1, dil, rows, gw), lambda b, c: (b, 0, c, 2)),
            pl.BlockSpec((1, dil, W, gw), lambda b, c: (b, 0, prev_tile(c), 1)),
            pl.BlockSpec((1, dil, W, gw), lambda b, c: (b, 0, prev_tile(c), 2)),
            pl.BlockSpec(bias.shape, lambda b, c: (0, 0, 0)),
        ],
        out_specs=[out_spec, out_spec],
        out_shape=[out, out],
        compiler_params=_params("parallel", "arbitrary"),
        name=f"dilated_attention_d{dil}",
    )(qkv, qkv, qkv, qkv, qkv, bias)


def _dilated_bias(group, dil):
    W = B_WINDOW_STEPS
    n_heads = len(B_GROUP_DILATIONS) * B_HEADS_PER_GROUP
    idx = jnp.arange(1, n_heads + 1, dtype=F32)
    slopes = jnp.exp2(-8.0 * idx / n_heads)[group * B_HEADS_PER_GROUP:
                                             (group + 1) * B_HEADS_PER_GROUP]
    dist = jnp.arange(W)[:, None] + W - jnp.arange(2 * W)[None, :]
    band = (dist >= 0) & (dist <= W)
    bias = -slopes[:, None, None] * (dil * dist).astype(F32)
    return jnp.where(band[None], bias * LOG2E, NEG_BIG)


def _tiles(T, S):
    return dict(
        row_tm=min(512, S),
        dil_span={d: min(S, max(1024, B_WINDOW_STEPS * d)) for d in B_GROUP_DILATIONS},
    )


def kernel(x, a_w_in, a_b_f, a_w_out, b_w_q, b_w_out, kv_norm_g, w_kv, mix_norm_g,
           ffn_norm_g, ffn_w_up, ffn_conv_w, ffn_conv_b, ffn_w_down, final_norm_g):
    B, S, D = x.shape
    T = B * S
    t = _tiles(T, S)
    depth = mix_norm_g.shape[0]
    n_a = a_w_in.shape[0]
    qk_scale = HEAD_DIM ** -0.5
    row = lambda v: v.reshape(1, -1).astype(F32)

    xs = x.reshape(T, D)
    kv_w = None
    for layer in range(depth):
        if layer < n_a:
            hw = A_HEADS * HEAD_DIM
            w_in = a_w_in[layer]
            w_qk = w_in[:, :2 * hw].astype(BF16)
            w_vt = w_in[:, 2 * hw:3 * hw].astype(BF16).T
            w_f = jnp.pad(w_in[:, 3 * hw:], ((0, 0), (0, LANES - A_HEADS))).astype(BF16)
            b_f = jnp.pad(a_b_f[layer].astype(F32), (0, LANES - A_HEADS)).reshape(1, LANES)
            qk, vt, qaug, kaug = fox_proj(
                xs, row(mix_norm_g[layer]), w_qk, w_vt, w_f, b_f, tm=t["row_tm"], seq_len=S,
                n_heads=A_HEADS, q_cols=hw, scale=qk_scale * LOG2E, log_base=LOG2E)
            o = fox_attention(qk.reshape(B, S, 2 * hw), vt, qaug.reshape(B, S, hw),
                              kaug.reshape(B, S, hw))
            mixed, w_out = [o.reshape(T, -1)], a_w_out[layer].astype(BF16)
        else:
            bl = layer - n_a
            if kv_w is None:
                kv_w = w_kv.astype(BF16)
            qkv = b_proj(xs, row(mix_norm_g[layer]), row(kv_norm_g), b_w_q[bl].astype(BF16), kv_w,
                         batch=B, tm=t["row_tm"], dils=B_GROUP_DILATIONS, scale=qk_scale * LOG2E)
            outs, lses = [], []
            for g, dil in enumerate(B_GROUP_DILATIONS):
                o_g, lse_g = dilated_attention(qkv[g], _dilated_bias(g, dil), dil=dil,
                                               rows=t["dil_span"][dil] // dil)
                outs.append(o_g)
                lses.append(lse_g)
            mixed, w_out = outs + lses, b_w_out[bl].astype(BF16)
        last = layer == depth - 1
        xs = mixer_out_conv_ffn(
            xs, mixed, w_out, row(ffn_norm_g[layer]), ffn_w_up[layer].astype(BF16),
            ffn_conv_w[layer].astype(F32), row(ffn_conv_b[layer]),
            ffn_w_down[layer].astype(BF16), seq_len=S, tm=t["row_tm"],
            final_g=row(final_norm_g) if last else None)
    return xs.reshape(B, S, D)
```

```python
import functools

import numpy as np
import jax
import jax.numpy as jnp
from jax import lax
from jax.experimental import pallas as pl
from jax.experimental.pallas import tpu as pltpu

F32 = jnp.float32
BF16 = jnp.bfloat16

RMS_EPS = 1e-6
HEAD_DIM = 64
LANES = 128
SUBLANES = 8
A_HEADS = 16
B_GROUP_DILATIONS = (1, 4, 16)
B_HEADS_PER_GROUP = 8
B_WINDOW_STEPS = 128
CONV_WIDTH = 3
NEG_BIG = -1e30
VMEM_LIMIT_BYTES = 56 * 1024 * 1024


def _params(*semantics):
    return pltpu.CompilerParams(dimension_semantics=semantics,
                                vmem_limit_bytes=VMEM_LIMIT_BYTES)


def _rms_norm(x, g):
    y = x * lax.rsqrt(jnp.mean(x * x, axis=-1, keepdims=True) + RMS_EPS)
    return y * g


def _dot(a, b):
    return jnp.dot(a, b, preferred_element_type=F32)


def _dot_nt(a, b):
    return lax.dot_general(a, b, (((1,), (1,)), ((), ())), preferred_element_type=F32)


def _b_proj_kernel(x_ref, gq_ref, gkv_ref, wq_ref, wkv_ref, *rest, dils, scale):
    outs = rest[:len(dils)]
    hq_ref, hkv_ref = rest[len(dils):]
    tm, D = x_ref.shape
    gw = wq_ref.shape[1] // len(dils)
    n_slab = D // LANES
    x = x_ref[...]
    xn = x * lax.rsqrt(jnp.mean(x * x, axis=-1, keepdims=True) + RMS_EPS)
    hq = xn * gq_ref[...]
    hkv = xn * gkv_ref[...]
    for s in range(n_slab):
        hq_ref[s] = hq[:, s * LANES:(s + 1) * LANES]
        hkv_ref[s] = hkv[:, s * LANES:(s + 1) * LANES]

    def residue_major(h_ref, d):
        rows = tm // d
        return jnp.concatenate(
            [jnp.concatenate([h_ref[s, pl.ds(r, rows, stride=d), :] for s in range(n_slab)],
                             axis=1) for r in range(d)], axis=0).astype(BF16)

    for g, d in enumerate(dils):
        if d == 1:
            a_q, a_kv = hq.astype(BF16), hkv.astype(BF16)
        else:
            a_q, a_kv = residue_major(hq_ref, d), residue_major(hkv_ref, d)
        q = (_dot(a_q, wq_ref[:, g * gw:(g + 1) * gw]) * scale).astype(BF16)
        n_g = len(dils)
        k = _dot(a_kv, wkv_ref[:, g * gw:(g + 1) * gw]).astype(BF16)
        v = _dot(a_kv, wkv_ref[:, (n_g + g) * gw:(n_g + g + 1) * gw]).astype(BF16)
        rows = tm // d
        for r in range(d):
            outs[g][0, r, :, :gw] = q[r * rows:(r + 1) * rows]
            outs[g][0, r, :, gw:2 * gw] = k[r * rows:(r + 1) * rows]
            outs[g][0, r, :, 2 * gw:] = v[r * rows:(r + 1) * rows]


def b_proj(x, g_q, g_kv, w_q, w_kv, *, batch, tm, dils, scale):
    T, D = x.shape
    S = T // batch
    tps = S // tm
    gw = w_q.shape[1] // len(dils)
    resident = dict(pipeline_mode=pl.Buffered(1))
    return pl.pallas_call(
        functools.partial(_b_proj_kernel, dils=dils, scale=scale),
        grid=(T // tm,),
        in_specs=[pl.BlockSpec((tm, D), lambda i: (i, 0)),
                  pl.BlockSpec((1, D), lambda i: (0, 0)),
                  pl.BlockSpec((1, D), lambda i: (0, 0)),
                  pl.BlockSpec(w_q.shape, lambda i: (0, 0), **resident),
                  pl.BlockSpec(w_kv.shape, lambda i: (0, 0), **resident)],
        out_specs=[pl.BlockSpec((1, d, tm // d, 3 * gw), lambda i: (i // tps, 0, i % tps, 0))
                   for d in dils],
        out_shape=[jax.ShapeDtypeStruct((batch, d, S // d, 3 * gw), BF16) for d in dils],
        scratch_shapes=[pltpu.VMEM((D // LANES, tm, LANES), F32),
                        pltpu.VMEM((D // LANES, tm, LANES), F32)],
        compiler_params=_params("parallel"),
        name="b_proj",
    )(x, g_q, g_kv, w_q, w_kv)


def _split3(x):
    hi = x.astype(BF16)
    rem = x - hi.astype(F32)
    mid = rem.astype(BF16)
    return hi, mid, (rem - mid.astype(F32)).astype(BF16)


def _fox_proj_kernel(x_ref, g_ref, wqk_ref, wvt_ref, wf_ref, bf_ref, eq_ref, ek_ref, cq_ref,
                     ck_ref, qk_ref, vt_ref, qaug_ref, kaug_ref, carry_ref,
                     *, q_cols, scale, log_base, n_heads, tiles_per_seq):
    tm = x_ref.shape[0]

    @pl.when(pl.program_id(0) == 0)
    def _():
        carry_ref[...] = jnp.zeros_like(carry_ref)

    h = _rms_norm(x_ref[...], g_ref[...]).astype(BF16)

    z = _dot(h, wf_ref[...]) + bf_ref[...]
    logf = jnp.minimum(z, 0.0) - jnp.log(1.0 + jnp.exp(-jnp.abs(z)))
    r = lax.broadcasted_iota(jnp.int32, (tm, tm), 0)
    c = lax.broadcasted_iota(jnp.int32, (tm, tm), 1)
    tri = jnp.where(r >= c, 1.0, 0.0).astype(BF16)
    p_hi, p_mid, p_lo = _split3(logf)
    first = (pl.program_id(0) % tiles_per_seq) == 0
    prev = jnp.where(first, 0.0, carry_ref[SUBLANES - 1:SUBLANES, :])
    cum = _dot(tri, p_hi) + _dot(tri, p_mid) + _dot(tri, p_lo) + prev
    carry_ref[...] = cum[tm - SUBLANES:]
    hi, mid, lo = (t.astype(F32) for t in _split3(cum * log_base))
    lane = lax.broadcasted_iota(jnp.int32, cum.shape, 1)
    packed = jnp.where(lane < n_heads, hi,
                       jnp.where(lane < 2 * n_heads, pltpu.roll(mid, n_heads, axis=1),
                                 pltpu.roll(lo, 2 * n_heads, axis=1))).astype(BF16)
    qaug_ref[...] = (_dot(packed, eq_ref[...]) + cq_ref[...]).astype(BF16)
    kaug_ref[...] = (_dot(packed, ek_ref[...]) + ck_ref[...]).astype(BF16)

    qk_ref[:, :q_cols] = (_dot(h, wqk_ref[:, :q_cols]) * scale).astype(BF16)
    qk_ref[:, q_cols:] = _dot(h, wqk_ref[:, q_cols:]).astype(BF16)
    vt_ref[...] = _dot_nt(wvt_ref[...], h).astype(BF16)


def _aug_placement(n_heads):
    assert 3 * n_heads <= LANES
    width = n_heads // 2 * LANES
    eq = np.zeros((LANES, width), np.float32)
    ek = np.zeros((LANES, width), np.float32)
    cq = np.zeros((1, width), np.float32)
    ck = np.zeros((1, width), np.float32)
    for hd in range(n_heads):
        base = hd // 2 * LANES + (HEAD_DIM if hd % 2 == 0 else 0)
        for part in range(3):
            ek[part * n_heads + hd, base + part] = 1.0
            eq[part * n_heads + hd, base + 3 + part] = 1.0
            cq[0, base + part] = -1.0
            ck[0, base + 3 + part] = 1.0
    return (jnp.asarray(eq, BF16), jnp.asarray(ek, BF16), jnp.asarray(cq), jnp.asarray(ck))


def fox_proj(x, g, w_qk, w_vt, w_f, b_f, *, tm, seq_len, n_heads, q_cols, scale, log_base):
    T, D = x.shape
    n_qk = w_qk.shape[1]
    n_v = w_vt.shape[0]
    aug_w = n_heads // 2 * LANES
    resident = dict(pipeline_mode=pl.Buffered(1))
    const = lambda shape: pl.BlockSpec(shape, lambda i: (0, 0))
    return pl.pallas_call(
        functools.partial(_fox_proj_kernel, q_cols=q_cols, scale=scale, log_base=log_base,
                          n_heads=n_heads,
                          tiles_per_seq=seq_len // tm),
        grid=(T // tm,),
        in_specs=[pl.BlockSpec((tm, D), lambda i: (i, 0)),
                  const((1, D)),
                  pl.BlockSpec((D, n_qk), lambda i: (0, 0), **resident),
                  pl.BlockSpec((n_v, D), lambda i: (0, 0), **resident),
                  const((D, LANES)), const((1, LANES)),
                  const((LANES, aug_w)), const((LANES, aug_w)),
                  const((1, aug_w)), const((1, aug_w))],
        out_specs=[pl.BlockSpec((tm, n_qk), lambda i: (i, 0)),
                   pl.BlockSpec((n_v, tm), lambda i: (0, i)),
                   pl.BlockSpec((tm, aug_w), lambda i: (i, 0)),
                   pl.BlockSpec((tm, aug_w), lambda i: (i, 0))],
        out_shape=[jax.ShapeDtypeStruct((T, n_qk), BF16),
                   jax.ShapeDtypeStruct((n_v, T), BF16),
                   jax.ShapeDtypeStruct((T, aug_w), BF16),
                   jax.ShapeDtypeStruct((T, aug_w), BF16)],
        scratch_shapes=[pltpu.VMEM((SUBLANES, LANES), F32)],
        compiler_params=_params("arbitrary"),
        name="fox_proj",
    )(x, g, w_qk, w_vt, w_f, b_f, *_aug_placement(n_heads))


FOX_TQ = 4096
FOX_TK = 256
FOX_DEN_ROWS = 16
LOG2E = 1.4426950408889634


def _row_groups(x):
    return [x[r * SUBLANES:(r + 1) * SUBLANES] for r in range(x.shape[0] // SUBLANES)]


def _tree(op, xs):
    while len(xs) > 1:
        xs = [op(xs[a], xs[a + 1]) for a in range(0, len(xs) - 1, 2)] + (
            [xs[-1]] if len(xs) % 2 else [])
    return xs[0]


def _sublane_allreduce(op, x):
    for shift in (4, 2, 1):
        x = op(x, pltpu.roll(x, shift, axis=0))
    return x


def _tile_rows(x, reps):
    return jnp.concatenate([x] * reps, axis=0)


def _fox_kernel(q_ref, k_ref, vt_ref, qaug_ref, kaug_ref, o_ref,
                qa_ref, ka_ref, m_ref, acc_ref, st_ref, p_ref, alpha_ref):
    S = q_ref.shape[1]
    tq, tk = min(FOX_TQ, S), FOX_TK
    lane = lax.broadcasted_iota(jnp.int32, (tq, LANES), 1)

    def augment(c, carry):
        rows = pl.ds(pl.multiple_of(c * tq, tq), tq)
        for hh in range(2):
            in_head = (lane < HEAD_DIM) if hh == 0 else (lane >= HEAD_DIM)
            qa_ref[hh, rows, :] = jnp.where(in_head, q_ref[0, rows, :], qaug_ref[0, rows, :])
            ka_ref[hh, rows, :] = jnp.where(in_head, k_ref[0, rows, :], kaug_ref[0, rows, :])
        return carry

    lax.fori_loop(0, S // tq, augment, 0)
    acc_rows = acc_ref.shape[1]
    ones = jnp.ones((acc_rows - HEAD_DIM, tk), BF16)


    def scores(i, j, lo=0):
        q0 = pl.multiple_of(i * tq + lo, tk)
        k0 = pl.multiple_of(j * tk, tk)
        return [_dot_nt(ka_ref[hh, pl.ds(k0, tk), :], qa_ref[hh, pl.ds(q0, tq - lo), :])
                for hh in range(2)]

    def weighted_values(j, par, lo=0):
        k0 = pl.multiple_of(j * tk, tk)
        return [_dot(jnp.concatenate(
            [vt_ref[hh * HEAD_DIM:(hh + 1) * HEAD_DIM, pl.ds(k0, tk)], ones], axis=0),
            p_ref[par, hh, :, lo:]) for hh in range(2)]

    def accumulate(pv, lo=0):
        for hh in range(2):
            acc_ref[hh, :, lo:] = (
                acc_ref[hh, :, lo:] * _tile_rows(alpha_ref[hh, :, lo:], acc_rows // SUBLANES)
                + pv[hh])

    def step(i, j, par, lo=0, prev_lo=0, diagonal=False, next_lo=None):
        nxt = scores(i, j + 1, next_lo) if next_lo is not None else None
        pv = weighted_values(jnp.maximum(j - 1, 0), 1 - par, prev_lo)
        alphas = []
        for hh in range(2):
            st = st_ref[par, hh, :, lo:]
            if diagonal:
                key = lax.broadcasted_iota(jnp.int32, st.shape, 0)
                qry = lax.broadcasted_iota(jnp.int32, st.shape, 1)
                st = jnp.where(key <= qry, st, NEG_BIG)
            m_old = m_ref[hh, :, lo:]
            m_new = jnp.maximum(m_old, _sublane_allreduce(
                jnp.maximum, _tree(jnp.maximum, _row_groups(st))))
            alphas.append(jnp.exp2(m_old - m_new))
            m_ref[hh, :, lo:] = m_new
            p_ref[par, hh, :, lo:] = jnp.exp2(
                st - _tile_rows(m_new, tk // SUBLANES)).astype(BF16)
        if nxt is not None:
            for hh in range(2):
                st_ref[1 - par, hh, :, next_lo:] = nxt[hh]
        accumulate(pv, prev_lo)
        for hh in range(2):
            alpha_ref[hh, :, lo:] = alphas[hh]

    per_q = tq // tk
    assert per_q % 2 == 0

    def kv_body(i, jj, carry):
        for dj in range(per_q):
            step(i, jj * per_q + dj, dj % 2, next_lo=0)
        return carry

    def q_tile(i, carry):
        m_ref[...] = jnp.full(m_ref.shape, NEG_BIG, F32)
        acc_ref[...] = jnp.zeros(acc_ref.shape, F32)
        p_ref[...] = jnp.zeros(p_ref.shape, BF16)
        alpha_ref[...] = jnp.ones(alpha_ref.shape, F32)
        first = scores(i, 0)
        for hh in range(2):
            st_ref[0, hh] = first[hh]
        lax.fori_loop(0, i, functools.partial(kv_body, i), 0)
        for dj in range(per_q):
            step(i, i * per_q + dj, dj % 2, lo=dj * tk, prev_lo=max(dj - 1, 0) * tk, diagonal=True,
                 next_lo=(dj + 1) * tk if dj + 1 < per_q else None)
        last_lo = (per_q - 1) * tk
        accumulate(weighted_values((i + 1) * per_q - 1, (per_q - 1) % 2, last_lo), last_lo)
        outs = []
        for hh in range(2):
            l = acc_ref[hh, HEAD_DIM:HEAD_DIM + SUBLANES]
            outs.append(acc_ref[hh, :HEAD_DIM] / _tile_rows(l, HEAD_DIM // SUBLANES))
        ot = jnp.concatenate(outs, axis=0)
        o_ref[0, pl.ds(pl.multiple_of(i * tq, tq), tq), :] = ot.T.astype(o_ref.dtype)
        return carry

    lax.fori_loop(0, S // tq, q_tile, 0)


def fox_attention(qk, vt, qaug, kaug):
    B, S, W2 = qk.shape
    npair = W2 // 2 // LANES
    t = min(FOX_TQ, S)
    return pl.pallas_call(
        _fox_kernel,
        grid=(B, npair),
        in_specs=[
            pl.BlockSpec((1, S, LANES), lambda b, h: (b, 0, h)),
            pl.BlockSpec((1, S, LANES), lambda b, h: (b, 0, npair + h)),
            pl.BlockSpec((LANES, S), lambda b, h: (h, b)),
            pl.BlockSpec((1, S, LANES), lambda b, h: (b, 0, h)),
            pl.BlockSpec((1, S, LANES), lambda b, h: (b, 0, h)),
        ],
        out_specs=pl.BlockSpec((1, S, LANES), lambda b, h: (b, 0, h)),
        out_shape=jax.ShapeDtypeStruct((B, S, npair * LANES), BF16),
        scratch_shapes=[pltpu.VMEM((2, S, LANES), BF16),
                        pltpu.VMEM((2, S, LANES), BF16),
                        pltpu.VMEM((2, SUBLANES, t), F32),
                        pltpu.VMEM((2, HEAD_DIM + FOX_DEN_ROWS, t), F32),
                        pltpu.VMEM((2, 2, FOX_TK, t), F32),
                        pltpu.VMEM((2, 2, FOX_TK, t), BF16),
                        pltpu.VMEM((2, SUBLANES, t), F32)],
        compiler_params=_params("parallel", "parallel"),
        name="fox_attention",
    )(qk, qk, vt, qaug, kaug)


FFN_CHUNK = 256


def _shift_rows(u, prev, shift):
    rolled = pltpu.roll(u, shift, axis=0)
    head = jnp.concatenate([prev, u[:SUBLANES]], axis=0)
    top = pltpu.roll(head, shift, axis=0)[SUBLANES:]
    return jnp.concatenate([top, rolled[SUBLANES:]], axis=0)


def _merged_groups(refs):
    wide = lambda ref: jnp.concatenate([ref[0, hp] for hp in range(ref.shape[1])], axis=1)
    n = len(refs) // 2
    outs, lses = [wide(r) for r in refs[:n]], [wide(r) for r in refs[n:]]
    m = functools.reduce(jnp.maximum, lses)
    es = [jnp.exp2(l - m) for l in lses]
    return sum(e * o for e, o in zip(es, outs)) / sum(es)


def _ffn_kernel(x_ref, *rest, n_mix, d_ff, tiles_per_seq, final_norm):
    mix_refs, rest = rest[:n_mix], rest[n_mix:]
    wo_ref, g_ref, wup_ref, cw_ref, cb_ref, wd_ref = rest[:6]
    rest = rest[6:]
    if final_norm:
        gf_ref, o_ref, h_ref, act_ref, carry_ref = rest
    else:
        o_ref, h_ref, act_ref, carry_ref = rest
    tm = x_ref.shape[0]
    first = (pl.program_id(0) % tiles_per_seq) == 0

    @pl.when(pl.program_id(0) == 0)
    def _():
        carry_ref[...] = jnp.zeros_like(carry_ref)

    mixed = mix_refs[0][...] if n_mix == 1 else _merged_groups(mix_refs).astype(BF16)
    x = x_ref[...] + _dot(mixed, wo_ref[...])
    h_ref[...] = _rms_norm(x, g_ref[...]).astype(BF16)

    def conv_branch(col):
        sl = slice(col, col + FFN_CHUNK)
        u = _dot(h_ref[...], wup_ref[:, sl])
        prev = jnp.where(first, 0.0, carry_ref[:, sl])
        carry_ref[:, sl] = u[tm - SUBLANES:]
        u1 = _shift_rows(u, prev, 1)
        u2 = _shift_rows(u, prev, 2)
        return (cw_ref[0:1, sl] * u2 + cw_ref[1:2, sl] * u1 + cw_ref[2:3, sl] * u
                + cb_ref[:, sl])

    for c in range(d_ff // FFN_CHUNK):
        a = conv_branch(c * FFN_CHUNK)
        gate = conv_branch(d_ff + c * FFN_CHUNK)
        act = gate * (1.0 / (1.0 + jnp.exp(-gate))) * a
        act_ref[:, c * FFN_CHUNK:(c + 1) * FFN_CHUNK] = act.astype(BF16)

    y = x + _dot(act_ref[...], wd_ref[...])
    if final_norm:
        y = _rms_norm(y, gf_ref[...])
    o_ref[...] = y


def mixer_out_conv_ffn(x, mixed, w_out, g, w_up, conv_w, conv_b, w_down, *, seq_len, tm,
                       final_g=None):
    T, D = x.shape
    d_ff = w_down.shape[0]
    tps = seq_len // tm
    resident = dict(pipeline_mode=pl.Buffered(1))
    if len(mixed) == 1:
        mix_specs = [pl.BlockSpec((tm, mixed[0].shape[1]), lambda i: (i, 0))]
    else:
        npair = mixed[0].shape[1]
        mix_specs = [pl.BlockSpec((1, npair, tm, LANES),
                                  lambda i: (i // tps, 0, i % tps, 0))] * len(mixed)
    in_specs = [pl.BlockSpec((tm, D), lambda i: (i, 0))] + mix_specs + [
        pl.BlockSpec(w_out.shape, lambda i: (0, 0), **resident),
        pl.BlockSpec((1, D), lambda i: (0, 0)),
        pl.BlockSpec((D, 2 * d_ff), lambda i: (0, 0), **resident),
        pl.BlockSpec((CONV_WIDTH, 2 * d_ff), lambda i: (0, 0)),
        pl.BlockSpec((1, 2 * d_ff), lambda i: (0, 0)),
        pl.BlockSpec((d_ff, D), lambda i: (0, 0), **resident),
    ]
    args = [x, *mixed, w_out, g, w_up, conv_w, conv_b, w_down]
    if final_g is not None:
        in_specs.append(pl.BlockSpec((1, D), lambda i: (0, 0)))
        args.append(final_g)
    kern = functools.partial(_ffn_kernel, n_mix=len(mixed), d_ff=d_ff, tiles_per_seq=tps,
                             final_norm=final_g is not None)
    return pl.pallas_call(
        kern,
        grid=(T // tm,),
        in_specs=in_specs,
        out_specs=pl.BlockSpec((tm, D), lambda i: (i, 0)),
        out_shape=jax.ShapeDtypeStruct((T, D), F32),
        scratch_shapes=[pltpu.VMEM((tm, D), BF16),
                        pltpu.VMEM((tm, d_ff), BF16),
                        pltpu.VMEM((SUBLANES, 2 * d_ff), F32)],
        compiler_params=_params("arbitrary"),
        name="conv_ffn_final" if final_g is not None else "conv_ffn",
    )(*args)


DILATED_TILES_PER_TRIP = 8


def _dilated_kernel(q_ref, k_ref, v_ref, kp_ref, vp_ref, bias_ref, o_ref, lse_ref,
                    *, dil, rows):
    W = B_WINDOW_STEPS
    first_chunk = pl.program_id(1) == 0
    lane = lax.broadcasted_iota(jnp.int32, (W, LANES), 1)
    col2 = lax.broadcasted_iota(jnp.int32, (2 * W, 2 * W), 1)
    npair = q_ref.shape[3] // LANES
    tiles = rows // W

    def tile(tt, carry):
        r = tt // tiles
        n = tt % tiles
        r0 = pl.multiple_of(n * W, W)
        prev = pl.ds(pl.multiple_of(jnp.maximum(n - 1, 0) * W, W), W)
        dst = pl.ds(r0, W) if dil == 1 else pl.ds(r0 * dil + r, W, stride=dil)
        no_prev = jnp.logical_and(first_chunk, n == 0)
        kill_prev2 = jnp.logical_and(no_prev, col2 < W)
        for hp in range(npair):
            ls = slice(hp * LANES, (hp + 1) * LANES)
            q = q_ref[0, r, pl.ds(r0, W), ls]
            kk = jnp.concatenate([jnp.where(n == 0, kp_ref[0, r, :, ls], k_ref[0, r, prev, ls]),
                                  k_ref[0, r, pl.ds(r0, W), ls]], axis=0)
            vv = jnp.concatenate([jnp.where(n == 0, vp_ref[0, r, :, ls], v_ref[0, r, prev, ls]),
                                  v_ref[0, r, pl.ds(r0, W), ls]], axis=0)
            zero = jnp.zeros_like(q)
            qs = jnp.concatenate([jnp.where(lane < HEAD_DIM, q, zero),
                                  jnp.where(lane >= HEAD_DIM, q, zero)], axis=0)
            bias = bias_ref[pl.ds(2 * hp, 2)].reshape(2 * W, 2 * W)
            bias = jnp.where(kill_prev2, NEG_BIG, bias)
            s = _dot_nt(qs, kk) + bias
            m = jnp.max(s, axis=-1, keepdims=True)
            p = jnp.exp2(s - m)
            l = jnp.sum(p, axis=-1, keepdims=True)
            o = _dot(p.astype(BF16), vv) / l
            lse = m + jnp.log2(l)
            o_ref[0, hp, dst, :] = jnp.where(lane < HEAD_DIM, o[:W], o[W:])
            lse_ref[0, hp, dst, :] = jnp.where(lane < HEAD_DIM, lse[:W], lse[W:])
        return carry

    def tile_group(tg, carry):
        for u in range(DILATED_TILES_PER_TRIP):
            tile(tg * DILATED_TILES_PER_TRIP + u, carry)
        return carry

    lax.fori_loop(0, dil * tiles // DILATED_TILES_PER_TRIP, tile_group, 0)


def dilated_attention(qkv, bias, *, dil, rows):
    B, _, L, w3 = qkv.shape
    gw = w3 // 3
    W = B_WINDOW_STEPS
    npair = gw // LANES
    tiles = rows // W
    span = rows * dil

    def prev_tile(c):
        return jnp.maximum(c * tiles - 1, 0)

    out = jax.ShapeDtypeStruct((B, npair, L * dil, LANES), F32)
    out_spec = pl.BlockSpec((1, npair, span, LANES), lambda b, c: (b, 0, c, 0))
    return pl.pallas_call(
        functools.partial(_dilated_kernel, dil=dil, rows=rows),
        grid=(B, L // rows),
        in_specs=[
            pl.BlockSpec((1, dil, rows, gw), lambda b, c: (b, 0, c, 0)),
            pl.BlockSpec((1, dil, rows, gw), lambda b, c: (b, 0, c, 1)),
            pl.BlockSpec((1, dil, rows, gw), lambda b, c: (b, 0, c, 2)),
            pl.BlockSpec((1, dil, W, gw), lambda b, c: (b, 0, prev_tile(c), 1)),
            pl.BlockSpec((1, dil, W, gw), lambda b, c: (b, 0, prev_tile(c), 2)),
            pl.BlockSpec(bias.shape, lambda b, c: (0, 0, 0)),
        ],
        out_specs=[out_spec, out_spec],
        out_shape=[out, out],
        compiler_params=_params("parallel", "arbitrary"),
        name=f"dilated_attention_d{dil}",
    )(qkv, qkv, qkv, qkv, qkv, bias)


def _dilated_bias(group, dil):
    W = B_WINDOW_STEPS
    n_heads = len(B_GROUP_DILATIONS) * B_HEADS_PER_GROUP
    idx = jnp.arange(1, n_heads + 1, dtype=F32)
    slopes = jnp.exp2(-8.0 * idx / n_heads)[group * B_HEADS_PER_GROUP:
                                             (group + 1) * B_HEADS_PER_GROUP]
    dist = jnp.arange(W)[:, None] + W - jnp.arange(2 * W)[None, :]
    band = (dist >= 0) & (dist <= W)
    bias = -slopes[:, None, None] * (dil * dist).astype(F32)
    return jnp.where(band[None], bias * LOG2E, NEG_BIG)


def _tiles(T, S):
    return dict(
        row_tm=min(512, S),
        ffn_tm_single=min(1024, S),
        dil_span={d: min(S, max(1024, B_WINDOW_STEPS * d)) for d in B_GROUP_DILATIONS},
    )


def kernel(x, a_w_in, a_b_f, a_w_out, b_w_q, b_w_out, kv_norm_g, w_kv, mix_norm_g,
           ffn_norm_g, ffn_w_up, ffn_conv_w, ffn_conv_b, ffn_w_down, final_norm_g):
    B, S, D = x.shape
    T = B * S
    t = _tiles(T, S)
    depth = mix_norm_g.shape[0]
    n_a = a_w_in.shape[0]
    qk_scale = HEAD_DIM ** -0.5
    row = lambda v: v.reshape(1, -1).astype(F32)

    xs = x.reshape(T, D)
    kv_w = None
    for layer in range(depth):
        if layer < n_a:
            hw = A_HEADS * HEAD_DIM
            w_in = a_w_in[layer]
            w_qk = w_in[:, :2 * hw].astype(BF16)
            w_vt = w_in[:, 2 * hw:3 * hw].astype(BF16).T
            w_f = jnp.pad(w_in[:, 3 * hw:], ((0, 0), (0, LANES - A_HEADS))).astype(BF16)
            b_f = jnp.pad(a_b_f[layer].astype(F32), (0, LANES - A_HEADS)).reshape(1, LANES)
            qk, vt, qaug, kaug = fox_proj(
                xs, row(mix_norm_g[layer]), w_qk, w_vt, w_f, b_f, tm=t["row_tm"], seq_len=S,
                n_heads=A_HEADS, q_cols=hw, scale=qk_scale * LOG2E, log_base=LOG2E)
            o = fox_attention(qk.reshape(B, S, 2 * hw), vt, qaug.reshape(B, S, hw),
                              kaug.reshape(B, S, hw))
            mixed, w_out = [o.reshape(T, -1)], a_w_out[layer].astype(BF16)
        else:
            bl = layer - n_a
            if kv_w is None:
                kv_w = w_kv.astype(BF16)
            qkv = b_proj(xs, row(mix_norm_g[layer]), row(kv_norm_g), b_w_q[bl].astype(BF16), kv_w,
                         batch=B, tm=t["row_tm"], dils=B_GROUP_DILATIONS, scale=qk_scale * LOG2E)
            outs, lses = [], []
            for g, dil in enumerate(B_GROUP_DILATIONS):
                o_g, lse_g = dilated_attention(qkv[g], _dilated_bias(g, dil), dil=dil,
                                               rows=t["dil_span"][dil] // dil)
                outs.append(o_g)
                lses.append(lse_g)
            mixed, w_out = outs + lses, b_w_out[bl].astype(BF16)
        last = layer == depth - 1
        xs = mixer_out_conv_ffn(
            xs, mixed, w_out, row(ffn_norm_g[layer]), ffn_w_up[layer].astype(BF16),
            ffn_conv_w[layer].astype(F32), row(ffn_conv_b[layer]),
            ffn_w_down[layer].astype(BF16), seq_len=S,
            tm=t["ffn_tm_single" if len(mixed) == 1 else "row_tm"],
            final_g=row(final_norm_g) if last else None)
    return xs.reshape(B, S, D)
```

```python
import functools

import numpy as np
import jax
import jax.numpy as jnp
from jax import lax
from jax.experimental import pallas as pl
from jax.experimental.pallas import tpu as pltpu

F32 = jnp.float32
BF16 = jnp.bfloat16

RMS_EPS = 1e-6
HEAD_DIM = 64
LANES = 128
SUBLANES = 8
A_HEADS = 16
B_GROUP_DILATIONS = (1, 4, 16)
B_HEADS_PER_GROUP = 8
B_WINDOW_STEPS = 128
CONV_WIDTH = 3
NEG_BIG = -1e30
VMEM_LIMIT_BYTES = 56 * 1024 * 1024


def _params(*semantics):
    return pltpu.CompilerParams(dimension_semantics=semantics,
                                vmem_limit_bytes=VMEM_LIMIT_BYTES)


def _rms_norm(x, g):
    y = x * lax.rsqrt(jnp.mean(x * x, axis=-1, keepdims=True) + RMS_EPS)
    return y * g


def _dot(a, b):
    return jnp.dot(a, b, preferred_element_type=F32)


def _dot_nt(a, b):
    return lax.dot_general(a, b, (((1,), (1,)), ((), ())), preferred_element_type=F32)


def _b_proj_kernel(x_ref, gq_ref, gkv_ref, wq_ref, wkv_ref, *rest, dils, scale):
    outs = rest[:len(dils)]
    hq_ref, hkv_ref = rest[len(dils):]
    tm, D = x_ref.shape
    gw = wq_ref.shape[1] // len(dils)
    n_slab = D // LANES
    x = x_ref[...]
    xn = x * lax.rsqrt(jnp.mean(x * x, axis=-1, keepdims=True) + RMS_EPS)
    hq = xn * gq_ref[...]
    hkv = xn * gkv_ref[...]
    for s in range(n_slab):
        hq_ref[s] = hq[:, s * LANES:(s + 1) * LANES]
        hkv_ref[s] = hkv[:, s * LANES:(s + 1) * LANES]

    def residue_major(h_ref, d):
        rows = tm // d
        return jnp.concatenate(
            [jnp.concatenate([h_ref[s, pl.ds(r, rows, stride=d), :] for s in range(n_slab)],
                             axis=1) for r in range(d)], axis=0).astype(BF16)

    for g, d in enumerate(dils):
        if d == 1:
            a_q, a_kv = hq.astype(BF16), hkv.astype(BF16)
        else:
            a_q, a_kv = residue_major(hq_ref, d), residue_major(hkv_ref, d)
        q = (_dot(a_q, wq_ref[:, g * gw:(g + 1) * gw]) * scale).astype(BF16)
        n_g = len(dils)
        k = _dot(a_kv, wkv_ref[:, g * gw:(g + 1) * gw]).astype(BF16)
        v = _dot(a_kv, wkv_ref[:, (n_g + g) * gw:(n_g + g + 1) * gw]).astype(BF16)
        rows = tm // d
        for r in range(d):
            outs[g][0, r, :, :gw] = q[r * rows:(r + 1) * rows]
            outs[g][0, r, :, gw:2 * gw] = k[r * rows:(r + 1) * rows]
            outs[g][0, r, :, 2 * gw:] = v[r * rows:(r + 1) * rows]


def b_proj(x, g_q, g_kv, w_q, w_kv, *, batch, tm, dils, scale):
    T, D = x.shape
    S = T // batch
    tps = S // tm
    gw = w_q.shape[1] // len(dils)
    resident = dict(pipeline_mode=pl.Buffered(1))
    return pl.pallas_call(
        functools.partial(_b_proj_kernel, dils=dils, scale=scale),
        grid=(T // tm,),
        in_specs=[pl.BlockSpec((tm, D), lambda i: (i, 0)),
                  pl.BlockSpec((1, D), lambda i: (0, 0)),
                  pl.BlockSpec((1, D), lambda i: (0, 0)),
                  pl.BlockSpec(w_q.shape, lambda i: (0, 0), **resident),
                  pl.BlockSpec(w_kv.shape, lambda i: (0, 0), **resident)],
        out_specs=[pl.BlockSpec((1, d, tm // d, 3 * gw), lambda i: (i // tps, 0, i % tps, 0))
                   for d in dils],
        out_shape=[jax.ShapeDtypeStruct((batch, d, S // d, 3 * gw), BF16) for d in dils],
        scratch_shapes=[pltpu.VMEM((D // LANES, tm, LANES), F32),
                        pltpu.VMEM((D // LANES, tm, LANES), F32)],
        compiler_params=_params("parallel"),
        name="b_proj",
    )(x, g_q, g_kv, w_q, w_kv)


def _split3(x):
    hi = x.astype(BF16)
    rem = x - hi.astype(F32)
    mid = rem.astype(BF16)
    return hi, mid, (rem - mid.astype(F32)).astype(BF16)


def _fox_proj_kernel(x_ref, g_ref, wqk_ref, wvt_ref, wf_ref, bf_ref, eq_ref, ek_ref, cq_ref,
                     ck_ref, qk_ref, vt_ref, qaug_ref, kaug_ref, carry_ref,
                     *, q_cols, scale, log_base, n_heads, tiles_per_seq):
    tm = x_ref.shape[0]

    @pl.when(pl.program_id(0) == 0)
    def _():
        carry_ref[...] = jnp.zeros_like(carry_ref)

    h = _rms_norm(x_ref[...], g_ref[...]).astype(BF16)

    z = _dot(h, wf_ref[...]) + bf_ref[...]
    logf = jnp.minimum(z, 0.0) - jnp.log(1.0 + jnp.exp(-jnp.abs(z)))
    r = lax.broadcasted_iota(jnp.int32, (tm, tm), 0)
    c = lax.broadcasted_iota(jnp.int32, (tm, tm), 1)
    tri = jnp.where(r >= c, 1.0, 0.0).astype(BF16)
    p_hi, p_mid, p_lo = _split3(logf)
    first = (pl.program_id(0) % tiles_per_seq) == 0
    prev = jnp.where(first, 0.0, carry_ref[SUBLANES - 1:SUBLANES, :])
    cum = _dot(tri, p_hi) + _dot(tri, p_mid) + _dot(tri, p_lo) + prev
    carry_ref[...] = cum[tm - SUBLANES:]
    hi, mid, lo = (t.astype(F32) for t in _split3(cum * log_base))
    lane = lax.broadcasted_iota(jnp.int32, cum.shape, 1)
    packed = jnp.where(lane < n_heads, hi,
                       jnp.where(lane < 2 * n_heads, pltpu.roll(mid, n_heads, axis=1),
                                 pltpu.roll(lo, 2 * n_heads, axis=1))).astype(BF16)
    qaug_ref[...] = (_dot(packed, eq_ref[...]) + cq_ref[...]).astype(BF16)
    kaug_ref[...] = (_dot(packed, ek_ref[...]) + ck_ref[...]).astype(BF16)

    qk_ref[:, :q_cols] = (_dot(h, wqk_ref[:, :q_cols]) * scale).astype(BF16)
    qk_ref[:, q_cols:] = _dot(h, wqk_ref[:, q_cols:]).astype(BF16)
    vt_ref[...] = _dot_nt(wvt_ref[...], h).astype(BF16)


def _aug_placement(n_heads):
    assert 3 * n_heads <= LANES
    width = n_heads // 2 * LANES
    eq = np.zeros((LANES, width), np.float32)
    ek = np.zeros((LANES, width), np.float32)
    cq = np.zeros((1, width), np.float32)
    ck = np.zeros((1, width), np.float32)
    for hd in range(n_heads):
        base = hd // 2 * LANES + (HEAD_DIM if hd % 2 == 0 else 0)
        for part in range(3):
            ek[part * n_heads + hd, base + part] = 1.0
            eq[part * n_heads + hd, base + 3 + part] = 1.0
            cq[0, base + part] = -1.0
            ck[0, base + 3 + part] = 1.0
    return (jnp.asarray(eq, BF16), jnp.asarray(ek, BF16), jnp.asarray(cq), jnp.asarray(ck))


def fox_proj(x, g, w_qk, w_vt, w_f, b_f, *, tm, seq_len, n_heads, q_cols, scale, log_base):
    T, D = x.shape
    n_qk = w_qk.shape[1]
    n_v = w_vt.shape[0]
    aug_w = n_heads // 2 * LANES
    resident = dict(pipeline_mode=pl.Buffered(1))
    const = lambda shape: pl.BlockSpec(shape, lambda i: (0, 0))
    return pl.pallas_call(
        functools.partial(_fox_proj_kernel, q_cols=q_cols, scale=scale, log_base=log_base,
                          n_heads=n_heads,
                          tiles_per_seq=seq_len // tm),
        grid=(T // tm,),
        in_specs=[pl.BlockSpec((tm, D), lambda i: (i, 0)),
                  const((1, D)),
                  pl.BlockSpec((D, n_qk), lambda i: (0, 0), **resident),
                  pl.BlockSpec((n_v, D), lambda i: (0, 0), **resident),
                  const((D, LANES)), const((1, LANES)),
                  const((LANES, aug_w)), const((LANES, aug_w)),
                  const((1, aug_w)), const((1, aug_w))],
        out_specs=[pl.BlockSpec((tm, n_qk), lambda i: (i, 0)),
                   pl.BlockSpec((n_v, tm), lambda i: (0, i)),
                   pl.BlockSpec((tm, aug_w), lambda i: (i, 0)),
                   pl.BlockSpec((tm, aug_w), lambda i: (i, 0))],
        out_shape=[jax.ShapeDtypeStruct((T, n_qk), BF16),
                   jax.ShapeDtypeStruct((n_v, T), BF16),
                   jax.ShapeDtypeStruct((T, aug_w), BF16),
                   jax.ShapeDtypeStruct((T, aug_w), BF16)],
        scratch_shapes=[pltpu.VMEM((SUBLANES, LANES), F32)],
        compiler_params=_params("arbitrary"),
        name="fox_proj",
    )(x, g, w_qk, w_vt, w_f, b_f, *_aug_placement(n_heads))


FOX_TQ = 4096
FOX_TK = 256
FOX_DEN_ROWS = 16
LOG2E = 1.4426950408889634


def _row_groups(x):
    return [x[r * SUBLANES:(r + 1) * SUBLANES] for r in range(x.shape[0] // SUBLANES)]


def _tree(op, xs):
    while len(xs) > 1:
        xs = [op(xs[a], xs[a + 1]) for a in range(0, len(xs) - 1, 2)] + (
            [xs[-1]] if len(xs) % 2 else [])
    return xs[0]


def _sublane_allreduce(op, x):
    for shift in (4, 2, 1):
        x = op(x, pltpu.roll(x, shift, axis=0))
    return x


def _tile_rows(x, reps):
    return jnp.concatenate([x] * reps, axis=0)


def _fox_kernel(q_ref, k_ref, vt_ref, qaug_ref, kaug_ref, o_ref,
                qa_ref, ka_ref, m_ref, acc_ref, st_ref, p_ref, alpha_ref):
    S = q_ref.shape[1]
    tq, tk = min(FOX_TQ, S), FOX_TK
    lane = lax.broadcasted_iota(jnp.int32, (tq, LANES), 1)

    def augment(c, carry):
        rows = pl.ds(pl.multiple_of(c * tq, tq), tq)
        for hh in range(2):
            in_head = (lane < HEAD_DIM) if hh == 0 else (lane >= HEAD_DIM)
            qa_ref[hh, rows, :] = jnp.where(in_head, q_ref[0, rows, :], qaug_ref[0, rows, :])
            ka_ref[hh, rows, :] = jnp.where(in_head, k_ref[0, rows, :], kaug_ref[0, rows, :])
        return carry

    lax.fori_loop(0, S // tq, augment, 0)
    acc_rows = acc_ref.shape[1]
    ones = jnp.ones((acc_rows - HEAD_DIM, tk), BF16)


    def scores(i, j, lo=0):
        q0 = pl.multiple_of(i * tq + lo, tk)
        k0 = pl.multiple_of(j * tk, tk)
        return [_dot_nt(ka_ref[hh, pl.ds(k0, tk), :], qa_ref[hh, pl.ds(q0, tq - lo), :])
                for hh in range(2)]

    def weighted_values(j, par, lo=0):
        k0 = pl.multiple_of(j * tk, tk)
        return [_dot(jnp.concatenate(
            [vt_ref[hh * HEAD_DIM:(hh + 1) * HEAD_DIM, pl.ds(k0, tk)], ones], axis=0),
            p_ref[par, hh, :, lo:]) for hh in range(2)]

    def accumulate(pv, lo=0):
        for hh in range(2):
            acc_ref[hh, :, lo:] = (
                acc_ref[hh, :, lo:] * _tile_rows(alpha_ref[hh, :, lo:], acc_rows // SUBLANES)
                + pv[hh])

    def step(i, j, par, lo=0, prev_lo=0, diagonal=False, next_lo=None):
        nxt = scores(i, j + 1, next_lo) if next_lo is not None else None
        pv = weighted_values(jnp.maximum(j - 1, 0), 1 - par, prev_lo)
        alphas = []
        for hh in range(2):
            st = st_ref[par, hh, :, lo:]
            if diagonal:
                key = lax.broadcasted_iota(jnp.int32, st.shape, 0)
                qry = lax.broadcasted_iota(jnp.int32, st.shape, 1)
                st = jnp.where(key <= qry, st, NEG_BIG)
            m_old = m_ref[hh, :, lo:]
            m_new = jnp.maximum(m_old, _sublane_allreduce(
                jnp.maximum, _tree(jnp.maximum, _row_groups(st))))
            alphas.append(jnp.exp2(m_old - m_new))
            m_ref[hh, :, lo:] = m_new
            p_ref[par, hh, :, lo:] = jnp.exp2(
                st - _tile_rows(m_new, tk // SUBLANES)).astype(BF16)
        if nxt is not None:
            for hh in range(2):
                st_ref[1 - par, hh, :, next_lo:] = nxt[hh]
        accumulate(pv, prev_lo)
        for hh in range(2):
            alpha_ref[hh, :, lo:] = alphas[hh]

    per_q = tq // tk
    assert per_q % 2 == 0

    def kv_body(i, jj, carry):
        for dj in range(per_q):
            step(i, jj * per_q + dj, dj % 2, next_lo=0)
        return carry

    def q_tile(i, carry):
        m_ref[...] = jnp.full(m_ref.shape, NEG_BIG, F32)
        acc_ref[...] = jnp.zeros(acc_ref.shape, F32)
        p_ref[...] = jnp.zeros(p_ref.shape, BF16)
        alpha_ref[...] = jnp.ones(alpha_ref.shape, F32)
        first = scores(i, 0)
        for hh in range(2):
            st_ref[0, hh] = first[hh]
        lax.fori_loop(0, i, functools.partial(kv_body, i), 0)
        for dj in range(per_q):
            step(i, i * per_q + dj, dj % 2, lo=dj * tk, prev_lo=max(dj - 1, 0) * tk, diagonal=True,
                 next_lo=(dj + 1) * tk if dj + 1 < per_q else None)
        last_lo = (per_q - 1) * tk
        accumulate(weighted_values((i + 1) * per_q - 1, (per_q - 1) % 2, last_lo), last_lo)
        outs = []
        for hh in range(2):
            l = acc_ref[hh, HEAD_DIM:HEAD_DIM + SUBLANES]
            outs.append(acc_ref[hh, :HEAD_DIM] / _tile_rows(l, HEAD_DIM // SUBLANES))
        ot = jnp.concatenate(outs, axis=0)
        o_ref[0, pl.ds(pl.multiple_of(i * tq, tq), tq), :] = ot.T.astype(o_ref.dtype)
        return carry

    lax.fori_loop(0, S // tq, q_tile, 0)


def fox_attention(qk, vt, qaug, kaug):
    B, S, W2 = qk.shape
    npair = W2 // 2 // LANES
    t = min(FOX_TQ, S)
    return pl.pallas_call(
        _fox_kernel,
        grid=(B, npair),
        in_specs=[
            pl.BlockSpec((1, S, LANES), lambda b, h: (b, 0, h)),
            pl.BlockSpec((1, S, LANES), lambda b, h: (b, 0, npair + h)),
            pl.BlockSpec((LANES, S), lambda b, h: (h, b)),
            pl.BlockSpec((1, S, LANES), lambda b, h: (b, 0, h)),
            pl.BlockSpec((1, S, LANES), lambda b, h: (b, 0, h)),
        ],
        out_specs=pl.BlockSpec((1, S, LANES), lambda b, h: (b, 0, h)),
        out_shape=jax.ShapeDtypeStruct((B, S, npair * LANES), BF16),
        scratch_shapes=[pltpu.VMEM((2, S, LANES), BF16),
                        pltpu.VMEM((2, S, LANES), BF16),
                        pltpu.VMEM((2, SUBLANES, t), F32),
                        pltpu.VMEM((2, HEAD_DIM + FOX_DEN_ROWS, t), F32),
                        pltpu.VMEM((2, 2, FOX_TK, t), F32),
                        pltpu.VMEM((2, 2, FOX_TK, t), BF16),
                        pltpu.VMEM((2, SUBLANES, t), F32)],
        compiler_params=_params("parallel", "parallel"),
        name="fox_attention",
    )(qk, qk, vt, qaug, kaug)


FFN_CHUNK = 256


def _shift_rows(u, prev, shift):
    rolled = pltpu.roll(u, shift, axis=0)
    head = jnp.concatenate([prev, u[:SUBLANES]], axis=0)
    top = pltpu.roll(head, shift, axis=0)[SUBLANES:]
    return jnp.concatenate([top, rolled[SUBLANES:]], axis=0)


def _merged_groups(refs):
    wide = lambda ref: jnp.concatenate([ref[0, hp] for hp in range(ref.shape[1])], axis=1)
    n = len(refs) // 2
    outs, lses = [wide(r) for r in refs[:n]], [wide(r) for r in refs[n:]]
    m = functools.reduce(jnp.maximum, lses)
    es = [jnp.exp2(l - m) for l in lses]
    return sum(e * o for e, o in zip(es, outs)) / sum(es)


def _ffn_kernel(x_ref, *rest, n_mix, d_ff, tiles_per_seq, final_norm):
    mix_refs, rest = rest[:n_mix], rest[n_mix:]
    wo_ref, g_ref, wup_ref, cw_ref, cb_ref, wd_ref = rest[:6]
    rest = rest[6:]
    if final_norm:
        gf_ref, o_ref, h_ref, act_ref, carry_ref = rest
    else:
        o_ref, h_ref, act_ref, carry_ref = rest
    tm = x_ref.shape[0]
    first = (pl.program_id(0) % tiles_per_seq) == 0

    @pl.when(pl.program_id(0) == 0)
    def _():
        carry_ref[...] = jnp.zeros_like(carry_ref)

    mixed = mix_refs[0][...] if n_mix == 1 else _merged_groups(mix_refs).astype(BF16)
    x = x_ref[...] + _dot(mixed, wo_ref[...])
    h_ref[...] = _rms_norm(x, g_ref[...]).astype(BF16)

    def conv_branch(col):
        sl = slice(col, col + FFN_CHUNK)
        u = _dot(h_ref[...], wup_ref[:, sl])
        prev = jnp.where(first, 0.0, carry_ref[:, sl])
        carry_ref[:, sl] = u[tm - SUBLANES:]
        u1 = _shift_rows(u, prev, 1)
        u2 = _shift_rows(u, prev, 2)
        return (cw_ref[0:1, sl] * u2 + cw_ref[1:2, sl] * u1 + cw_ref[2:3, sl] * u
                + cb_ref[:, sl])

    for c in range(d_ff // FFN_CHUNK):
        a = conv_branch(c * FFN_CHUNK)
        gate = conv_branch(d_ff + c * FFN_CHUNK)
        act = gate * (1.0 / (1.0 + jnp.exp(-gate))) * a
        act_ref[:, c * FFN_CHUNK:(c + 1) * FFN_CHUNK] = act.astype(BF16)

    y = x + _dot(act_ref[...], wd_ref[...])
    if final_norm:
        y = _rms_norm(y, gf_ref[...])
    o_ref[...] = y


def mixer_out_conv_ffn(x, mixed, w_out, g, w_up, conv_w, conv_b, w_down, *, seq_len, tm,
                       final_g=None):
    T, D = x.shape
    d_ff = w_down.shape[0]
    tps = seq_len // tm
    resident = dict(pipeline_mode=pl.Buffered(1))
    if len(mixed) == 1:
        mix_specs = [pl.BlockSpec((tm, mixed[0].shape[1]), lambda i: (i, 0))]
    else:
        npair = mixed[0].shape[1]
        mix_specs = [pl.BlockSpec((1, npair, tm, LANES),
                                  lambda i: (i // tps, 0, i % tps, 0))] * len(mixed)
    in_specs = [pl.BlockSpec((tm, D), lambda i: (i, 0))] + mix_specs + [
        pl.BlockSpec(w_out.shape, lambda i: (0, 0), **resident),
        pl.BlockSpec((1, D), lambda i: (0, 0)),
        pl.BlockSpec((D, 2 * d_ff), lambda i: (0, 0), **resident),
        pl.BlockSpec((CONV_WIDTH, 2 * d_ff), lambda i: (0, 0)),
        pl.BlockSpec((1, 2 * d_ff), lambda i: (0, 0)),
        pl.BlockSpec((d_ff, D), lambda i: (0, 0), **resident),
    ]
    args = [x, *mixed, w_out, g, w_up, conv_w, conv_b, w_down]
    if final_g is not None:
        in_specs.append(pl.BlockSpec((1, D), lambda i: (0, 0)))
        args.append(final_g)
    kern = functools.partial(_ffn_kernel, n_mix=len(mixed), d_ff=d_ff, tiles_per_seq=tps,
                             final_norm=final_g is not None)
    return pl.pallas_call(
        kern,
        grid=(T // tm,),
        in_specs=in_specs,
        out_specs=pl.BlockSpec((tm, D), lambda i: (i, 0)),
        out_shape=jax.ShapeDtypeStruct((T, D), F32),
        scratch_shapes=[pltpu.VMEM((tm, D), BF16),
                        pltpu.VMEM((tm, d_ff), BF16),
                        pltpu.VMEM((SUBLANES, 2 * d_ff), F32)],
        compiler_params=_params("arbitrary"),
        name="conv_ffn_final" if final_g is not None else "conv_ffn",
    )(*args)


DILATED_TILES_PER_TRIP = 8


def _dilated_kernel(q_ref, k_ref, v_ref, *rest, dil, rows, carry_halo):
    W = B_WINDOW_STEPS
    first_chunk = pl.program_id(1) == 0
    if carry_halo:
        bias_ref, o_ref, lse_ref, kp_ref, vp_ref = rest

        @pl.when(first_chunk)
        def _():
            kp_ref[...] = jnp.zeros_like(kp_ref)
            vp_ref[...] = jnp.zeros_like(vp_ref)
    else:
        kp_ref, vp_ref, bias_ref, o_ref, lse_ref = rest
    lane = lax.broadcasted_iota(jnp.int32, (W, LANES), 1)
    col2 = lax.broadcasted_iota(jnp.int32, (2 * W, 2 * W), 1)
    npair = q_ref.shape[3] // LANES
    tiles = rows // W

    def tile(tt, carry):
        r = tt // tiles
        n = tt % tiles
        r0 = pl.multiple_of(n * W, W)
        prev = pl.ds(pl.multiple_of(jnp.maximum(n - 1, 0) * W, W), W)
        dst = pl.ds(r0, W) if dil == 1 else pl.ds(r0 * dil + r, W, stride=dil)
        no_prev = jnp.logical_and(first_chunk, n == 0)
        kill_prev2 = jnp.logical_and(no_prev, col2 < W)
        for hp in range(npair):
            ls = slice(hp * LANES, (hp + 1) * LANES)
            q = q_ref[0, r, pl.ds(r0, W), ls]
            kk = jnp.concatenate([jnp.where(n == 0, kp_ref[0, r, :, ls], k_ref[0, r, prev, ls]),
                                  k_ref[0, r, pl.ds(r0, W), ls]], axis=0)
            vv = jnp.concatenate([jnp.where(n == 0, vp_ref[0, r, :, ls], v_ref[0, r, prev, ls]),
                                  v_ref[0, r, pl.ds(r0, W), ls]], axis=0)
            zero = jnp.zeros_like(q)
            qs = jnp.concatenate([jnp.where(lane < HEAD_DIM, q, zero),
                                  jnp.where(lane >= HEAD_DIM, q, zero)], axis=0)
            bias = bias_ref[pl.ds(2 * hp, 2)].reshape(2 * W, 2 * W)
            bias = jnp.where(kill_prev2, NEG_BIG, bias)
            s = _dot_nt(qs, kk) + bias
            m = jnp.max(s, axis=-1, keepdims=True)
            p = jnp.exp2(s - m)
            l = jnp.sum(p, axis=-1, keepdims=True)
            o = _dot(p.astype(BF16), vv) / l
            lse = m + jnp.log2(l)
            o_ref[0, hp, dst, :] = jnp.where(lane < HEAD_DIM, o[:W], o[W:])
            lse_ref[0, hp, dst, :] = jnp.where(lane < HEAD_DIM, lse[:W], lse[W:])
        return carry

    def tile_group(tg, carry):
        for u in range(DILATED_TILES_PER_TRIP):
            tile(tg * DILATED_TILES_PER_TRIP + u, carry)
        return carry

    lax.fori_loop(0, dil * tiles // DILATED_TILES_PER_TRIP, tile_group, 0)
    if carry_halo:
        kp_ref[...] = k_ref[...]
        vp_ref[...] = v_ref[...]


def dilated_attention(qkv, bias, *, dil, rows):
    B, _, L, w3 = qkv.shape
    gw = w3 // 3
    W = B_WINDOW_STEPS
    npair = gw // LANES
    tiles = rows // W
    span = rows * dil

    def prev_tile(c):
        return jnp.maximum(c * tiles - 1, 0)

    carry_halo = tiles == 1
    out = jax.ShapeDtypeStruct((B, npair, L * dil, LANES), F32)
    out_spec = pl.BlockSpec((1, npair, span, LANES), lambda b, c: (b, 0, c, 0))
    block = lambda lane_blk: pl.BlockSpec((1, dil, rows, gw), lambda b, c: (b, 0, c, lane_blk))
    halo = lambda lane_blk: pl.BlockSpec((1, dil, W, gw),
                                         lambda b, c: (b, 0, prev_tile(c), lane_blk))
    bias_spec = pl.BlockSpec(bias.shape, lambda b, c: (0, 0, 0))
    if carry_halo:
        in_specs, args = [block(0), block(1), block(2), bias_spec], (qkv, qkv, qkv, bias)
        scratch = [pltpu.VMEM((1, dil, W, gw), BF16), pltpu.VMEM((1, dil, W, gw), BF16)]
    else:
        in_specs = [block(0), block(1), block(2), halo(1), halo(2), bias_spec]
        args, scratch = (qkv, qkv, qkv, qkv, qkv, bias), []
    return pl.pallas_call(
        functools.partial(_dilated_kernel, dil=dil, rows=rows, carry_halo=carry_halo),
        grid=(B, L // rows),
        in_specs=in_specs,
        out_specs=[out_spec, out_spec],
        out_shape=[out, out],
        scratch_shapes=scratch,
        compiler_params=_params("parallel", "arbitrary"),
        name=f"dilated_attention_d{dil}",
    )(*args)


def _dilated_bias(group, dil):
    W = B_WINDOW_STEPS
    n_heads = len(B_GROUP_DILATIONS) * B_HEADS_PER_GROUP
    idx = jnp.arange(1, n_heads + 1, dtype=F32)
    slopes = jnp.exp2(-8.0 * idx / n_heads)[group * B_HEADS_PER_GROUP:
                                             (group + 1) * B_HEADS_PER_GROUP]
    dist = jnp.arange(W)[:, None] + W - jnp.arange(2 * W)[None, :]
    band = (dist >= 0) & (dist <= W)
    bias = -slopes[:, None, None] * (dil * dist).astype(F32)
    return jnp.where(band[None], bias * LOG2E, NEG_BIG)


def _tiles(T, S):
    return dict(
        row_tm=min(512, S),
        ffn_tm_single=min(1024, S),
        dil_span={d: min(S, max(1024, B_WINDOW_STEPS * d)) for d in B_GROUP_DILATIONS},
    )


def kernel(x, a_w_in, a_b_f, a_w_out, b_w_q, b_w_out, kv_norm_g, w_kv, mix_norm_g,
           ffn_norm_g, ffn_w_up, ffn_conv_w, ffn_conv_b, ffn_w_down, final_norm_g):
    B, S, D = x.shape
    T = B * S
    t = _tiles(T, S)
    depth = mix_norm_g.shape[0]
    n_a = a_w_in.shape[0]
    qk_scale = HEAD_DIM ** -0.5
    row = lambda v: v.reshape(1, -1).astype(F32)

    xs = x.reshape(T, D)
    kv_w = None
    for layer in range(depth):
        if layer < n_a:
            hw = A_HEADS * HEAD_DIM
            w_in = a_w_in[layer]
            w_qk = w_in[:, :2 * hw].astype(BF16)
            w_vt = w_in[:, 2 * hw:3 * hw].astype(BF16).T
            w_f = jnp.pad(w_in[:, 3 * hw:], ((0, 0), (0, LANES - A_HEADS))).astype(BF16)
            b_f = jnp.pad(a_b_f[layer].astype(F32), (0, LANES - A_HEADS)).reshape(1, LANES)
            qk, vt, qaug, kaug = fox_proj(
                xs, row(mix_norm_g[layer]), w_qk, w_vt, w_f, b_f, tm=t["row_tm"], seq_len=S,
                n_heads=A_HEADS, q_cols=hw, scale=qk_scale * LOG2E, log_base=LOG2E)
            o = fox_attention(qk.reshape(B, S, 2 * hw), vt, qaug.reshape(B, S, hw),
                              kaug.reshape(B, S, hw))
            mixed, w_out = [o.reshape(T, -1)], a_w_out[layer].astype(BF16)
        else:
            bl = layer - n_a
            if kv_w is None:
                kv_w = w_kv.astype(BF16)
            qkv = b_proj(xs, row(mix_norm_g[layer]), row(kv_norm_g), b_w_q[bl].astype(BF16), kv_w,
                         batch=B, tm=t["row_tm"], dils=B_GROUP_DILATIONS, scale=qk_scale * LOG2E)
            outs, lses = [], []
            for g, dil in enumerate(B_GROUP_DILATIONS):
                o_g, lse_g = dilated_attention(qkv[g], _dilated_bias(g, dil), dil=dil,
                                               rows=t["dil_span"][dil] // dil)
                outs.append(o_g)
                lses.append(lse_g)
            mixed, w_out = outs + lses, b_w_out[bl].astype(BF16)
        last = layer == depth - 1
        xs = mixer_out_conv_ffn(
            xs, mixed, w_out, row(ffn_norm_g[layer]), ffn_w_up[layer].astype(BF16),
            ffn_conv_w[layer].astype(F32), row(ffn_conv_b[layer]),
            ffn_w_down[layer].astype(BF16), seq_len=S,
            tm=t["ffn_tm_single" if len(mixed) == 1 else "row_tm"],
            final_g=row(final_norm_g) if last else None)
    return xs.reshape(B, S, D)
```
